```python
import math
import jax, jax.numpy as jnp
from jax import lax
import numpy as np

D_MODEL = 1024
BATCH = 8
SEQ = 2048
DEPTH = 4

PLE_DIM = 256
GRID_W = 64
Q_BLOCK = 128
ROPE_THETA = 10000.0
EPS = 1e-6
CONV_CH = D_MODEL // 4
CONV_WIDTH = 31
DIFF_HEADS = 4
DIFF_QK_DIM = 32
DIFF_V_DIM = 2 * DIFF_QK_DIM
GQA_HEADS = 8
GQA_KV_HEADS = 2
GQA_HEAD_DIM = 64
FFN_DIM = 2816
FFN_CONV_WIDTH = 3
N_BRANCH = 3

SEG_SIZES = (
    2 * CONV_CH,
    DIFF_HEADS * 2 * DIFF_QK_DIM,
    DIFF_HEADS * 2 * DIFF_QK_DIM,
    DIFF_HEADS * DIFF_V_DIM,
    GQA_HEADS * GQA_HEAD_DIM,
    GQA_KV_HEADS * GQA_HEAD_DIM,
    GQA_KV_HEADS * GQA_HEAD_DIM,
    N_BRANCH * D_MODEL,
)
IN_COLS = sum(SEG_SIZES)
SPLIT_POINTS = tuple(int(v) for v in np.cumsum(SEG_SIZES)[:-1])

kernel_name = "hybrid_gated_conv_diffattn_axialgqa_encoder"


def _rms(x, g):
    x32 = x.astype(jnp.float32)
    y = x32 * lax.rsqrt(jnp.mean(x32 * x32, axis=-1, keepdims=True) + EPS)
    return (y * g.astype(jnp.float32)).astype(x.dtype)


def _layernorm(x, g, b):
    x32 = x.astype(jnp.float32)
    mu = jnp.mean(x32, axis=-1, keepdims=True)
    var = jnp.mean(jnp.square(x32 - mu), axis=-1, keepdims=True)
    y = (x32 - mu) * lax.rsqrt(var + EPS)
    return (y * g.astype(jnp.float32) + b.astype(jnp.float32)).astype(x.dtype)


def _inv_freq(dim):
    return 1.0 / (ROPE_THETA ** (jnp.arange(0, dim, 2, dtype=jnp.float32) / dim))


def _rope(x, ang):
    d = x.shape[-1]
    cos = jnp.cos(ang)[None, :, None, :].astype(x.dtype)
    sin = jnp.sin(ang)[None, :, None, :].astype(x.dtype)
    xp = x.reshape(x.shape[:-1] + (d // 2, 2))
    x0, x1 = xp[..., 0], xp[..., 1]
    out = jnp.stack([x0 * cos - x1 * sin, x0 * sin + x1 * cos], axis=-1)
    return out.reshape(x.shape)


def _dwconv(u, w, b):
    width, ch = w.shape
    pad = width // 2
    y = lax.conv_general_dilated(
        u, w[:, None, :].astype(u.dtype), window_strides=(1,), padding=[(pad, pad)],
        dimension_numbers=("NWC", "WIO", "NWC"), feature_group_count=ch)
    return y + b.astype(u.dtype)


def _to_blocks(t):
    b, s = t.shape[:2]
    t = t.reshape((b, s // Q_BLOCK, Q_BLOCK) + t.shape[2:])
    return jnp.moveaxis(t, 1, 0)


def _from_blocks(t):
    t = jnp.moveaxis(t, 0, 1)
    return t.reshape((t.shape[0], t.shape[1] * t.shape[2]) + t.shape[3:])


def _diff_attention(q1, q2, k1, k2, v, lam):
    scale = DIFF_QK_DIM ** -0.5

    def blk(qs):
        qb1, qb2 = qs
        s1 = jnp.einsum("bqhd,bkhd->bhqk", qb1, k1).astype(jnp.float32) * scale
        s2 = jnp.einsum("bqhd,bkhd->bhqk", qb2, k2).astype(jnp.float32) * scale
        a = jax.nn.softmax(s1, axis=-1) - lam * jax.nn.softmax(s2, axis=-1)
        return jnp.einsum("bhqk,bkhd->bqhd", a.astype(v.dtype), v)

    out = lax.map(blk, (_to_blocks(q1), _to_blocks(q2)))
    return _from_blocks(out)


def _gqa_attention(q, k, v):
    b, s, hq, d = q.shape
    g = GQA_KV_HEADS
    r = hq // g
    scale = d ** -0.5
    qg = q.reshape(b, s, g, r, d)

    def blk(qb):
        sc = jnp.einsum("bqgrd,bkgd->bgrqk", qb, k).astype(jnp.float32) * scale
        pr = jax.nn.softmax(sc, axis=-1).astype(v.dtype)
        return jnp.einsum("bgrqk,bkgd->bqgrd", pr, v)

    out = _from_blocks(lax.map(blk, _to_blocks(qg)))
    return out.reshape(b, s, hq, d)


def setup_inputs(seed: int = 0) -> dict:
    key = jax.random.key(seed)
    ks = jax.random.split(key, 32)
    f32 = jnp.float32

    def w(k, shape, fan_in):
        return jax.random.normal(k, shape, f32) * (fan_in ** -0.5)

    def gain(k, shape):
        return 1.0 + 0.05 * jax.random.normal(k, shape, f32)

    def bias(k, shape):
        return 0.01 * jax.random.normal(k, shape, f32)

    L = DEPTH
    return {
        "x": jax.random.normal(ks[0], (BATCH, SEQ, D_MODEL), f32),
        "p": jax.random.normal(ks[1], (DEPTH, BATCH, SEQ, PLE_DIM), f32),
        "norm_mix_pre": gain(ks[2], (L, D_MODEL)),
        "norm_mix_post": gain(ks[3], (L, D_MODEL)),
        "w_in": w(ks[4], (L, D_MODEL, IN_COLS), D_MODEL),
        "conv_dw_w": w(ks[5], (L, CONV_WIDTH, CONV_CH), CONV_WIDTH),
        "conv_dw_b": bias(ks[6], (L, CONV_CH)),
        "conv_ln_g": gain(ks[7], (L, CONV_CH)),
        "conv_ln_b": bias(ks[8], (L, CONV_CH)),
        "w_conv_out": w(ks[9], (L, CONV_CH, D_MODEL), CONV_CH),
        "diff_lambda": 0.1 * jax.random.normal(ks[10], (L, 4, DIFF_QK_DIM), f32),
        "diff_subln_g": gain(ks[11], (L, DIFF_V_DIM)),
        "w_diff_out": w(ks[12], (L, DIFF_HEADS * DIFF_V_DIM, D_MODEL), DIFF_HEADS * DIFF_V_DIM),
        "gqa_q_norm": gain(ks[13], (L, GQA_HEAD_DIM)),
        "gqa_k_norm": gain(ks[14], (L, GQA_HEAD_DIM)),
        "w_gqa_out": w(ks[15], (L, GQA_HEADS * GQA_HEAD_DIM, D_MODEL), GQA_HEADS * GQA_HEAD_DIM),
        "w_out": w(ks[16], (L, D_MODEL, D_MODEL), D_MODEL),
        "norm_ffn_pre": gain(ks[17], (L, D_MODEL)),
        "norm_ffn_post": gain(ks[18], (L, D_MODEL)),
        "w_up": w(ks[19], (L, D_MODEL, 2 * FFN_DIM), D_MODEL),
        "ffn_dw_w": w(ks[20], (L, FFN_CONV_WIDTH, 2 * FFN_DIM), FFN_CONV_WIDTH),
        "ffn_dw_b": bias(ks[21], (L, 2 * FFN_DIM)),
        "w_down": w(ks[22], (L, FFN_DIM, D_MODEL), FFN_DIM),
        "w_ple": w(ks[23], (L, PLE_DIM, D_MODEL), PLE_DIM),
        "w_ple_gate": w(ks[24], (L, D_MODEL, D_MODEL), D_MODEL),
    }


def reference(x, p, norm_mix_pre, norm_mix_post, w_in, conv_dw_w, conv_dw_b, conv_ln_g,
              conv_ln_b, w_conv_out, diff_lambda, diff_subln_g, w_diff_out, gqa_q_norm,
              gqa_k_norm, w_gqa_out, w_out, norm_ffn_pre, norm_ffn_post, w_up, ffn_dw_w,
              ffn_dw_b, w_down, w_ple, w_ple_gate):
    B, S, D = x.shape
    ROWS = S // GRID_W
    t = jnp.arange(S, dtype=jnp.float32)
    ang_1d = t[:, None] * _inv_freq(DIFF_QK_DIM)[None, :]
    half = GQA_HEAD_DIM // 2
    row = jnp.repeat(jnp.arange(ROWS, dtype=jnp.float32), GRID_W)
    col = jnp.tile(jnp.arange(GRID_W, dtype=jnp.float32), ROWS)
    fr = _inv_freq(half)
    ang_2d = jnp.concatenate([row[:, None] * fr[None, :], col[:, None] * fr[None, :]], axis=-1)

    for i in range(DEPTH):
        h = _rms(x, norm_mix_pre[i])
        proj = h @ w_in[i]
        (a_in, dq, dk, dv, gq, gk, gv, gates) = jnp.split(proj, SPLIT_POINTS, axis=-1)

        a_val, a_gate = jnp.split(a_in, 2, axis=-1)
        u = a_val * jax.nn.sigmoid(a_gate)
        u = _dwconv(u, conv_dw_w[i], conv_dw_b[i])
        u = jax.nn.silu(_layernorm(u, conv_ln_g[i], conv_ln_b[i]))
        br_a = u @ w_conv_out[i]

        dq = dq.reshape(B, S, DIFF_HEADS, 2, DIFF_QK_DIM)
        dk = dk.reshape(B, S, DIFF_HEADS, 2, DIFF_QK_DIM)
        dv = dv.reshape(B, S, DIFF_HEADS, DIFF_V_DIM)
        q1, q2 = _rope(dq[..., 0, :], ang_1d), _rope(dq[..., 1, :], ang_1d)
        k1, k2 = _rope(dk[..., 0, :], ang_1d), _rope(dk[..., 1, :], ang_1d)
        lam_init = 0.8 - 0.6 * math.exp(-0.3 * i)
        lp = diff_lambda[i].astype(jnp.float32)
        lam = jnp.exp(jnp.sum(lp[0] * lp[1])) - jnp.exp(jnp.sum(lp[2] * lp[3])) + lam_init
        o_b = _diff_attention(q1, q2, k1, k2, dv, lam)
        o_b = _rms(o_b, diff_subln_g[i]) * (1.0 - lam_init)
        br_b = o_b.reshape(B, S, DIFF_HEADS * DIFF_V_DIM) @ w_diff_out[i]

        gq = _rms(gq.reshape(B, S, GQA_HEADS, GQA_HEAD_DIM), gqa_q_norm[i])
        gk = _rms(gk.reshape(B, S, GQA_KV_HEADS, GQA_HEAD_DIM), gqa_k_norm[i])
        gv = gv.reshape(B, S, GQA_KV_HEADS, GQA_HEAD_DIM)
        o_c = _gqa_attention(_rope(gq, ang_2d), _rope(gk, ang_2d), gv)
        br_c = o_c.reshape(B, S, GQA_HEADS * GQA_HEAD_DIM) @ w_gqa_out[i]

        g = jax.nn.sigmoid(gates).reshape(B, S, N_BRANCH, D)
        merged = g[:, :, 0] * br_a + g[:, :, 1] * br_b + g[:, :, 2] * br_c
        x = x + _rms(merged @ w_out[i], norm_mix_post[i])

        h2 = _rms(x, norm_ffn_pre[i])
        up = _dwconv(h2 @ w_up[i], ffn_dw_w[i], ffn_dw_b[i])
        f_gate, f_val = jnp.split(up, 2, axis=-1)
        ffn = (jax.nn.gelu(f_gate, approximate=True) * f_val) @ w_down[i]
        x = x + _rms(ffn, norm_ffn_post[i])

        x = x + jax.nn.sigmoid(x @ w_ple_gate[i]) * (p[i] @ w_ple[i])
    return x
```

```python
import functools
import math

import jax
import jax.numpy as jnp
from jax import lax
from jax.experimental import pallas as pl
from jax.experimental.pallas import tpu as pltpu

F32 = jnp.float32
BF16 = jnp.bfloat16

D_MODEL = 1024
SEQ = 2048
PLE_DIM = 256
GRID_W = 64
ROPE_THETA = 10000.0
EPS = 1e-6
CONV_CH = 256
CONV_WIDTH = 31
DIFF_HEADS = 4
DIFF_QK_DIM = 32
DIFF_V_DIM = 64
GQA_HEADS = 8
GQA_KV_HEADS = 2
GQA_HEAD_DIM = 64
FFN_DIM = 2816

C_CONV = 0
C_DQ = 512
C_DK = 768
C_DV = 1024
C_GQ = 1280
C_GK = 1792
C_GV = 1920
C_GATES = 2048
IN_COLS = 5120

LANES = 128
SUBLANES = 8
VMEM_LIMIT = 56 * 1024 * 1024

TM_IN = 512
TQ = 256
TM_MERGE = 512
TM_FFN = 512
FFN_CHUNK = 256
HALO = SUBLANES
CONV_ROWS = 128
CONV_PAD = 16


def _const_spec(shape):
    nd = len(shape)
    return pl.BlockSpec(shape, lambda *_: (0,) * nd, pipeline_mode=pl.Buffered(1))


def _params(n_axes):
    return pltpu.CompilerParams(
        dimension_semantics=("arbitrary",) * n_axes, vmem_limit_bytes=VMEM_LIMIT)


def _rms_rows(x, g):
    ms = jnp.mean(x * x, axis=-1, keepdims=True)
    return x * lax.rsqrt(ms + EPS) * g


def _rope_lanes(v, cos, sin_signed):
    outs = []
    for j in range(v.shape[1] // LANES):
        c = v[:, j * LANES:(j + 1) * LANES]
        nxt = pltpu.roll(c, LANES - 1, 1)
        prv = pltpu.roll(c, 1, 1)
        lane = lax.broadcasted_iota(jnp.int32, c.shape, 1)
        swapped = jnp.where(lane % 2 == 0, nxt, prv)
        outs.append(c * cos + swapped * sin_signed)
    return outs[0] if len(outs) == 1 else jnp.concatenate(outs, axis=1)


def _group_mean_sq(v, bd):
    sq = v * v
    hi = sq.astype(BF16)
    lo = (sq - hi.astype(F32)).astype(BF16)
    return (jnp.dot(hi, bd, preferred_element_type=F32)
            + jnp.dot(lo, bd, preferred_element_type=F32))


def _inproj_kernel(x_ref, gpre_ref, w_ref, cd_ref, sd_ref, cg_ref, sg_ref, gqn_ref, gkn_ref,
                   bdq_ref, bdk_ref,
                   u_ref, dqT_ref, dk_ref, dvT_ref, gqT_ref, gk_ref, gvT_ref, gates_ref):
    h = _rms_rows(x_ref[...], gpre_ref[...]).astype(BF16)

    def mm(lo, hi):
        return jnp.dot(h, w_ref[:, lo:hi], preferred_element_type=F32)

    a = mm(C_CONV, C_DQ)
    u_ref[...] = a[:, :CONV_CH] * jax.nn.sigmoid(a[:, CONV_CH:])

    cd, sd = cd_ref[...], sd_ref[...]
    dq = _rope_lanes(mm(C_DQ, C_DK), cd, sd) * (DIFF_QK_DIM ** -0.5)
    dqT_ref[...] = dq.T.astype(BF16)
    dk_ref[...] = _rope_lanes(mm(C_DK, C_DV), cd, sd).astype(BF16)
    dvT_ref[...] = mm(C_DV, C_GQ).T.astype(BF16)

    cg, sg = cg_ref[...], sg_ref[...]
    gq = mm(C_GQ, C_GK)
    gq = gq * lax.rsqrt(_group_mean_sq(gq, bdq_ref[...]) + EPS) * gqn_ref[...]
    gq = _rope_lanes(gq, cg, sg) * (GQA_HEAD_DIM ** -0.5)
    gqT_ref[...] = gq.T.astype(BF16)
    gk = mm(C_GK, C_GV)
    gk = gk * lax.rsqrt(_group_mean_sq(gk, bdk_ref[...]) + EPS) * gkn_ref[...]
    gk_ref[...] = _rope_lanes(gk, cg, sg).astype(BF16)
    gvT_ref[...] = mm(C_GV, C_GATES).T.astype(BF16)

    for j in range(3):
        lo = C_GATES + j * D_MODEL
        gates_ref[:, j * D_MODEL:(j + 1) * D_MODEL] = jax.nn.sigmoid(
            mm(lo, lo + D_MODEL)).astype(BF16)


def _inproj(x, gpre, w_in, tabs, gqn, gkn, bdq, bdk):
    T = x.shape[0]
    tm = TM_IN
    n_seq_tiles = SEQ // tm
    cd, sd, cg, sg = tabs
    row = lambda w: pl.BlockSpec((tm, w), lambda i: (i, 0))
    col = lambda h: pl.BlockSpec((h, tm), lambda i: (0, i))
    tab = pl.BlockSpec((tm, LANES), lambda i: (i % n_seq_tiles, 0))
    return pl.pallas_call(
        _inproj_kernel,
        grid=(T // tm,),
        in_specs=[row(D_MODEL), _const_spec((1, D_MODEL)), _const_spec((D_MODEL, IN_COLS)),
                  tab, tab, tab, tab,
                  _const_spec((1, 512)), _const_spec((1, 128)),
                  _const_spec((512, 512)), _const_spec((128, 128))],
        out_specs=[row(CONV_CH), col(256), row(256), col(256), col(512), row(128), col(128),
                   row(3 * D_MODEL)],
        out_shape=[
            jax.ShapeDtypeStruct((T, CONV_CH), F32),
            jax.ShapeDtypeStruct((256, T), BF16),
            jax.ShapeDtypeStruct((T, 256), BF16),
            jax.ShapeDtypeStruct((256, T), BF16),
            jax.ShapeDtypeStruct((512, T), BF16),
            jax.ShapeDtypeStruct((T, 128), BF16),
            jax.ShapeDtypeStruct((128, T), BF16),
            jax.ShapeDtypeStruct((T, 3 * D_MODEL), BF16),
        ],
        compiler_params=_params(1),
        name="inproj",
    )(x, gpre, w_in, cd, sd, cg, sg, gqn, gkn, bdq, bdk)


def _conv_kernel(u_ref, w_ref, b_ref, lng_ref, lnb_ref, o_ref, pad_ref):
    zeros = jnp.zeros((CONV_PAD, CONV_CH), F32)
    pad_ref[0:CONV_PAD, :] = zeros
    pad_ref[CONV_PAD + SEQ:CONV_PAD + SEQ + CONV_PAD, :] = zeros
    pad_ref[CONV_PAD:CONV_PAD + SEQ, :] = u_ref[...]
    w = w_ref[...]
    half = CONV_WIDTH // 2
    for c in range(SEQ // CONV_ROWS):
        base = c * CONV_ROWS + CONV_PAD - half
        acc = jnp.broadcast_to(b_ref[...], (CONV_ROWS, CONV_CH))
        for k in range(CONV_WIDTH):
            acc = acc + pad_ref[base + k:base + k + CONV_ROWS, :] * w[k:k + 1, :]
        mu = jnp.mean(acc, axis=-1, keepdims=True)
        d = acc - mu
        var = jnp.mean(d * d, axis=-1, keepdims=True)
        y = d * lax.rsqrt(var + EPS) * lng_ref[...] + lnb_ref[...]
        o_ref[c * CONV_ROWS:(c + 1) * CONV_ROWS, :] = (y * jax.nn.sigmoid(y)).astype(BF16)


def _conv_module(u, w, b, lng, lnb):
    T = u.shape[0]
    return pl.pallas_call(
        _conv_kernel,
        grid=(T // SEQ,),
        in_specs=[pl.BlockSpec((SEQ, CONV_CH), lambda i: (i, 0)),
                  _const_spec((CONV_WIDTH, CONV_CH)), _const_spec((1, CONV_CH)),
                  _const_spec((1, CONV_CH)), _const_spec((1, CONV_CH))],
        out_specs=pl.BlockSpec((SEQ, CONV_CH), lambda i: (i, 0)),
        out_shape=jax.ShapeDtypeStruct((T, CONV_CH), BF16),
        scratch_shapes=[pltpu.VMEM((SEQ + 2 * CONV_PAD, CONV_CH), F32)],
        compiler_params=_params(1),
        name="conv_module",
    )(u, w, b, lng, lnb)


def _softmax_t(k, wq):
    s = jnp.dot(k, wq, preferred_element_type=F32)
    p = jnp.exp(s - jnp.max(s, axis=0, keepdims=True))
    inv = 1.0 / jnp.sum(p, axis=0, keepdims=True)
    return p.astype(BF16), inv


def _place_rows(block, lo, total):
    rows, n = block.shape
    parts = []
    if lo > 0:
        parts.append(jnp.zeros((lo, n), block.dtype))
    parts.append(block)
    if total - lo - rows > 0:
        parts.append(jnp.zeros((total - lo - rows, n), block.dtype))
    return jnp.concatenate(parts, axis=0) if len(parts) > 1 else block


def _attn_kernel(lam_init, lam_ref, subg_ref, dqT_ref, dk_ref, dvT_ref, gqT_ref, gk_ref, gvT_ref,
                 ob_ref, oc_ref):
    lp = lam_ref[...]
    lam = (jnp.exp(jnp.sum(lp[0:1] * lp[1:2], keepdims=True))
           - jnp.exp(jnp.sum(lp[2:3] * lp[3:4], keepdims=True)) + lam_init)
    dk = dk_ref[...]
    heads = []
    for h in range(DIFF_HEADS):
        r0 = h * 2 * DIFF_QK_DIM
        w1 = _place_rows(dqT_ref[r0:r0 + DIFF_QK_DIM, :], r0, 2 * DIFF_HEADS * DIFF_QK_DIM)
        w2 = _place_rows(dqT_ref[r0 + DIFF_QK_DIM:r0 + 2 * DIFF_QK_DIM, :], r0 + DIFF_QK_DIM,
                         2 * DIFF_HEADS * DIFF_QK_DIM)
        vT = dvT_ref[h * DIFF_V_DIM:(h + 1) * DIFF_V_DIM, :]
        p1, inv1 = _softmax_t(dk, w1)
        p2, inv2 = _softmax_t(dk, w2)
        o = (jnp.dot(vT, p1, preferred_element_type=F32) * inv1
             - jnp.dot(vT, p2, preferred_element_type=F32) * (lam * inv2))
        ms = jnp.mean(o * o, axis=0, keepdims=True)
        heads.append(o * lax.rsqrt(ms + EPS) * subg_ref[...] * (1.0 - lam_init))
    ob_ref[...] = jnp.concatenate(heads, axis=0).T.astype(BF16)

    gk = gk_ref[...]
    heads = []
    per_group = GQA_HEADS // GQA_KV_HEADS
    for h in range(GQA_HEADS):
        g = h // per_group
        wq = _place_rows(gqT_ref[h * GQA_HEAD_DIM:(h + 1) * GQA_HEAD_DIM, :], g * GQA_HEAD_DIM,
                         GQA_KV_HEADS * GQA_HEAD_DIM)
        p, inv = _softmax_t(gk, wq)
        vT = gvT_ref[g * GQA_HEAD_DIM:(g + 1) * GQA_HEAD_DIM, :]
        heads.append(jnp.dot(vT, p, preferred_element_type=F32) * inv)
    oc_ref[...] = jnp.concatenate(heads, axis=0).T.astype(BF16)


def _attention(lam_init, lam_p, subg, dqT, dk, dvT, gqT, gk, gvT):
    T = dk.shape[0]
    nq = SEQ // TQ
    qcol = lambda h: pl.BlockSpec((h, TQ), lambda b, j: (0, b * nq + j))
    krow = lambda w: pl.BlockSpec((SEQ, w), lambda b, j: (b, 0))
    vcol = lambda h: pl.BlockSpec((h, SEQ), lambda b, j: (0, b))
    orow = lambda w: pl.BlockSpec((TQ, w), lambda b, j: (b * nq + j, 0))
    return pl.pallas_call(
        functools.partial(_attn_kernel, lam_init),
        grid=(T // SEQ, nq),
        in_specs=[_const_spec((4, DIFF_QK_DIM)), _const_spec((DIFF_V_DIM, 1)),
                  qcol(256), krow(256), vcol(256), qcol(512), krow(128), vcol(128)],
        out_specs=[orow(256), orow(512)],
        out_shape=[jax.ShapeDtypeStruct((T, 256), BF16), jax.ShapeDtypeStruct((T, 512), BF16)],
        compiler_params=_params(2),
        name="attention",
    )(lam_p, subg, dqT, dk, dvT, gqT, gk, gvT)


def _merge_kernel(x_ref, ua_ref, ob_ref, oc_ref, gates_ref, wc_ref, wd_ref, wg_ref, wo_ref,
                  gpost_ref, o_ref):
    def gate(j):
        return gates_ref[:, j * D_MODEL:(j + 1) * D_MODEL].astype(F32)

    merged = gate(0) * jnp.dot(ua_ref[...], wc_ref[...], preferred_element_type=F32)
    merged += gate(1) * jnp.dot(ob_ref[...], wd_ref[...], preferred_element_type=F32)
    merged += gate(2) * jnp.dot(oc_ref[...], wg_ref[...], preferred_element_type=F32)
    y = jnp.dot(merged.astype(BF16), wo_ref[...], preferred_element_type=F32)
    o_ref[...] = x_ref[...] + _rms_rows(y, gpost_ref[...])


def _merge(x, ua, ob, oc, gates, wc, wd, wg, wo, gpost):
    T = x.shape[0]
    tm = TM_MERGE
    row = lambda w: pl.BlockSpec((tm, w), lambda i: (i, 0))
    return pl.pallas_call(
        _merge_kernel,
        grid=(T // tm,),
        in_specs=[row(D_MODEL), row(256), row(256), row(512), row(3 * D_MODEL),
                  _const_spec((256, D_MODEL)), _const_spec((256, D_MODEL)),
                  _const_spec((512, D_MODEL)), _const_spec((D_MODEL, D_MODEL)),
                  _const_spec((1, D_MODEL))],
        out_specs=row(D_MODEL),
        out_shape=jax.ShapeDtypeStruct((T, D_MODEL), F32),
        compiler_params=_params(1),
        name="merge",
    )(x, ua, ob, oc, gates, wc, wd, wg, wo, gpost)


def _ffn_kernel(x_ref, xp_ref, xn_ref, p_ref, gpre_ref, wup_ref, dww_ref, dwb_ref, wdown_ref,
                gpost_ref, wple_ref, wpg_ref, o_ref, acc_ref):
    tm = TM_FFN
    i = pl.program_id(0)
    n_seq_tiles = SEQ // tm
    first = (i % n_seq_tiles) == 0
    last = (i % n_seq_tiles) == n_seq_tiles - 1
    x = x_ref[...]
    gpre = gpre_ref[...]
    hp = jnp.where(first, 0.0, _rms_rows(xp_ref[...], gpre))
    hn = jnp.where(last, 0.0, _rms_rows(xn_ref[...], gpre))
    h2 = jnp.concatenate([hp, _rms_rows(x, gpre), hn], axis=0).astype(BF16)

    def conv3(up, c0):
        w = dww_ref[:, c0:c0 + FFN_CHUNK]
        return (up[HALO - 1:HALO - 1 + tm] * w[0:1] + up[HALO:HALO + tm] * w[1:2]
                + up[HALO + 1:HALO + 1 + tm] * w[2:3] + dwb_ref[:, c0:c0 + FFN_CHUNK])

    for c in range(FFN_DIM // FFN_CHUNK):
        cg = c * FFN_CHUNK
        cv = FFN_DIM + c * FFN_CHUNK
        f_gate = conv3(jnp.dot(h2, wup_ref[:, cg:cg + FFN_CHUNK], preferred_element_type=F32), cg)
        f_val = conv3(jnp.dot(h2, wup_ref[:, cv:cv + FFN_CHUNK], preferred_element_type=F32), cv)
        act = (jax.nn.gelu(f_gate, approximate=True) * f_val).astype(BF16)
        part = jnp.dot(act, wdown_ref[cg:cg + FFN_CHUNK, :], preferred_element_type=F32)
        if c == 0:
            acc_ref[...] = part
        else:
            acc_ref[...] += part

    x2 = x + _rms_rows(acc_ref[...], gpost_ref[...])
    gate = jax.nn.sigmoid(jnp.dot(x2.astype(BF16), wpg_ref[...], preferred_element_type=F32))
    emb = jnp.dot(p_ref[...].astype(BF16), wple_ref[...], preferred_element_type=F32)
    o_ref[...] = x2 + gate * emb


def _ffn(x, p, gpre, wup, dww, dwb, wdown, gpost, wple, wpg):
    T = x.shape[0]
    tm = TM_FFN
    per = tm // HALO
    n_halo_blocks = T // HALO
    row = lambda w: pl.BlockSpec((tm, w), lambda i: (i, 0))
    prev = pl.BlockSpec((HALO, D_MODEL), lambda i: (jnp.maximum(i * per - 1, 0), 0))
    nxt = pl.BlockSpec((HALO, D_MODEL),
                       lambda i: (jnp.minimum((i + 1) * per, n_halo_blocks - 1), 0))
    return pl.pallas_call(
        _ffn_kernel,
        grid=(T // tm,),
        in_specs=[row(D_MODEL), prev, nxt, row(PLE_DIM), _const_spec((1, D_MODEL)),
                  _const_spec((D_MODEL, 2 * FFN_DIM)), _const_spec((3, 2 * FFN_DIM)),
                  _const_spec((1, 2 * FFN_DIM)), _const_spec((FFN_DIM, D_MODEL)),
                  _const_spec((1, D_MODEL)), _const_spec((PLE_DIM, D_MODEL)),
                  _const_spec((D_MODEL, D_MODEL))],
        out_specs=row(D_MODEL),
        out_shape=jax.ShapeDtypeStruct((T, D_MODEL), F32),
        scratch_shapes=[pltpu.VMEM((tm, D_MODEL), F32)],
        compiler_params=_params(1),
        name="ffn",
    )(x, x, x, p, gpre, wup, dww, dwb, wdown, gpost, wple, wpg)


def _inv_freq(dim):
    return 1.0 / (ROPE_THETA ** (jnp.arange(0, dim, 2, dtype=F32) / dim))


def _rope_tables():
    t = jnp.arange(SEQ, dtype=F32)
    ang_1d = t[:, None] * _inv_freq(DIFF_QK_DIM)[None, :]
    rows = SEQ // GRID_W
    row = jnp.repeat(jnp.arange(rows, dtype=F32), GRID_W)
    col = jnp.tile(jnp.arange(GRID_W, dtype=F32), rows)
    fr = _inv_freq(GQA_HEAD_DIM // 2)
    ang_2d = jnp.concatenate([row[:, None] * fr[None, :], col[:, None] * fr[None, :]], axis=-1)

    def lanes(ang):
        per_head = jnp.repeat(ang, 2, axis=1)
        reps = LANES // per_head.shape[1]
        cos = jnp.tile(jnp.cos(per_head), (1, reps))
        sign = jnp.tile(jnp.array([-1.0, 1.0], F32), LANES // 2)[None, :]
        sin = jnp.tile(jnp.sin(per_head), (1, reps)) * sign
        return cos, sin

    cd, sd = lanes(ang_1d)
    cg, sg = lanes(ang_2d)
    return cd, sd, cg, sg


def _block_diag_mean(n, group):
    idx = jnp.arange(n) // group
    return jnp.where(idx[:, None] == idx[None, :], 1.0 / group, 0.0).astype(BF16)


def kernel(x, p, norm_mix_pre, norm_mix_post, w_in, conv_dw_w, conv_dw_b, conv_ln_g, conv_ln_b, w_conv_out, diff_lambda, diff_subln_g, w_diff_out, gqa_q_norm, gqa_k_norm, w_gqa_out, w_out, norm_ffn_pre, norm_ffn_post, w_up, ffn_dw_w, ffn_dw_b, w_down, w_ple, w_ple_gate):
    B, S, D = x.shape
    depth = w_in.shape[0]
    assert (S, D) == (SEQ, D_MODEL)
    T = B * S
    tabs = _rope_tables()
    bdq = _block_diag_mean(GQA_HEADS * GQA_HEAD_DIM, GQA_HEAD_DIM)
    bdk = _block_diag_mean(GQA_KV_HEADS * GQA_HEAD_DIM, GQA_HEAD_DIM)
    xf = x.reshape(T, D)
    pf = p.reshape(depth, T, PLE_DIM)
    bf = lambda a: a.astype(BF16)
    r1 = lambda a: a.reshape(1, -1)
    for i in range(depth):
        lam_init = 0.8 - 0.6 * math.exp(-0.3 * i)
        u, dqT, dk, dvT, gqT, gk, gvT, gates = _inproj(
            xf, r1(norm_mix_pre[i]), bf(w_in[i]), tabs,
            r1(jnp.tile(gqa_q_norm[i], GQA_HEADS)), r1(jnp.tile(gqa_k_norm[i], GQA_KV_HEADS)),
            bdq, bdk)
        ua = _conv_module(u, conv_dw_w[i], r1(conv_dw_b[i]), r1(conv_ln_g[i]), r1(conv_ln_b[i]))
        ob, oc = _attention(lam_init, diff_lambda[i], diff_subln_g[i].reshape(-1, 1),
                            dqT, dk, dvT, gqT, gk, gvT)
        xf = _merge(xf, ua, ob, oc, gates, bf(w_conv_out[i]), bf(w_diff_out[i]),
                    bf(w_gqa_out[i]), bf(w_out[i]), r1(norm_mix_post[i]))
        xf = _ffn(xf, pf[i], r1(norm_ffn_pre[i]), bf(w_up[i]), ffn_dw_w[i], r1(ffn_dw_b[i]),
                  bf(w_down[i]), r1(norm_ffn_post[i]), bf(w_ple[i]), bf(w_ple_gate[i]))
    return xf.reshape(B, S, D)
```

```python
import functools
import math

import jax
import jax.numpy as jnp
from jax import lax
from jax.experimental import pallas as pl
from jax.experimental.pallas import tpu as pltpu

F32 = jnp.float32
BF16 = jnp.bfloat16

D_MODEL = 1024
SEQ = 2048
PLE_DIM = 256
GRID_W = 64
ROPE_THETA = 10000.0
EPS = 1e-6
CONV_CH = 256
CONV_WIDTH = 31
DIFF_HEADS = 4
DIFF_QK_DIM = 32
DIFF_V_DIM = 64
GQA_HEADS = 8
GQA_KV_HEADS = 2
GQA_HEAD_DIM = 64
FFN_DIM = 2816

C_CONV = 0
C_DQ = 512
C_DK = 768
C_DV = 1024
C_GQ = 1280
C_GK = 1792
C_GV = 1920
C_GATES = 2048
IN_COLS = 5120

LANES = 128
SUBLANES = 8
VMEM_LIMIT = 56 * 1024 * 1024

TM_IN = 512
TQ = 512
ONES_ROWS = 16
V_AUG = 64 + ONES_ROWS
LOG2E = math.log2(math.e)
TM_MERGE = 512
TM_FFN = 512
FFN_CHUNK = 256
HALO = SUBLANES
CONV_ROWS = 128
CONV_PAD = 16


def _const_spec(shape):
    nd = len(shape)
    return pl.BlockSpec(shape, lambda *_: (0,) * nd, pipeline_mode=pl.Buffered(1))


def _params(n_axes):
    return pltpu.CompilerParams(
        dimension_semantics=("arbitrary",) * n_axes, vmem_limit_bytes=VMEM_LIMIT)


def _rms_rows(x, g):
    ms = jnp.mean(x * x, axis=-1, keepdims=True)
    return x * lax.rsqrt(ms + EPS) * g


def _rope_lanes(v, cos, sin_signed):
    outs = []
    for j in range(v.shape[1] // LANES):
        c = v[:, j * LANES:(j + 1) * LANES]
        nxt = pltpu.roll(c, LANES - 1, 1)
        prv = pltpu.roll(c, 1, 1)
        lane = lax.broadcasted_iota(jnp.int32, c.shape, 1)
        swapped = jnp.where(lane % 2 == 0, nxt, prv)
        outs.append(c * cos + swapped * sin_signed)
    return outs[0] if len(outs) == 1 else jnp.concatenate(outs, axis=1)


def _group_mean_sq(v, bd):
    sq = v * v
    hi = sq.astype(BF16)
    lo = (sq - hi.astype(F32)).astype(BF16)
    return (jnp.dot(hi, bd, preferred_element_type=F32)
            + jnp.dot(lo, bd, preferred_element_type=F32))


def _with_ones_rows(vT, heads):
    ones = jnp.ones((ONES_ROWS, vT.shape[1]), vT.dtype)
    parts = []
    for h in range(heads):
        parts += [vT[h * 64:(h + 1) * 64], ones]
    return jnp.concatenate(parts, axis=0)


def _inproj_kernel(x_ref, gpre_ref, w_ref, cd_ref, sd_ref, cg_ref, sg_ref, gqn_ref, gkn_ref,
                   bdq_ref, bdk_ref,
                   u_ref, dqT_ref, dk_ref, dvT_ref, gqT_ref, gk_ref, gvT_ref, gates_ref):
    h = _rms_rows(x_ref[...], gpre_ref[...]).astype(BF16)

    def mm(lo, hi):
        return jnp.dot(h, w_ref[:, lo:hi], preferred_element_type=F32)

    a = mm(C_CONV, C_DQ)
    u_ref[...] = a[:, :CONV_CH] * jax.nn.sigmoid(a[:, CONV_CH:])

    cd, sd = cd_ref[...], sd_ref[...]
    dq = _rope_lanes(mm(C_DQ, C_DK), cd, sd) * (DIFF_QK_DIM ** -0.5 * LOG2E)
    dqT_ref[...] = dq.T.astype(BF16)
    dk_ref[...] = _rope_lanes(mm(C_DK, C_DV), cd, sd).astype(BF16)
    dvT_ref[...] = _with_ones_rows(mm(C_DV, C_GQ).T.astype(BF16), DIFF_HEADS)

    cg, sg = cg_ref[...], sg_ref[...]
    gq = mm(C_GQ, C_GK)
    gq = gq * lax.rsqrt(_group_mean_sq(gq, bdq_ref[...]) + EPS) * gqn_ref[...]
    gq = _rope_lanes(gq, cg, sg) * (GQA_HEAD_DIM ** -0.5 * LOG2E)
    gqT_ref[...] = gq.T.astype(BF16)
    gk = mm(C_GK, C_GV)
    gk = gk * lax.rsqrt(_group_mean_sq(gk, bdk_ref[...]) + EPS) * gkn_ref[...]
    gk_ref[...] = _rope_lanes(gk, cg, sg).astype(BF16)
    gvT_ref[...] = _with_ones_rows(mm(C_GV, C_GATES).T.astype(BF16), GQA_KV_HEADS)

    for j in range(3):
        lo = C_GATES + j * D_MODEL
        gates_ref[:, j * D_MODEL:(j + 1) * D_MODEL] = jax.nn.sigmoid(
            mm(lo, lo + D_MODEL)).astype(BF16)


def _inproj(x, gpre, w_in, tabs, gqn, gkn, bdq, bdk):
    T = x.shape[0]
    tm = TM_IN
    n_seq_tiles = SEQ // tm
    cd, sd, cg, sg = tabs
    row = lambda w: pl.BlockSpec((tm, w), lambda i: (i, 0))
    col = lambda h: pl.BlockSpec((h, tm), lambda i: (0, i))
    tab = pl.BlockSpec((tm, LANES), lambda i: (i % n_seq_tiles, 0))
    return pl.pallas_call(
        _inproj_kernel,
        grid=(T // tm,),
        in_specs=[row(D_MODEL), _const_spec((1, D_MODEL)), _const_spec((D_MODEL, IN_COLS)),
                  tab, tab, tab, tab,
                  _const_spec((1, 512)), _const_spec((1, 128)),
                  _const_spec((512, 512)), _const_spec((128, 128))],
        out_specs=[row(CONV_CH), col(256), row(256), col(DIFF_HEADS * V_AUG), col(512), row(128),
                   col(GQA_KV_HEADS * V_AUG), row(3 * D_MODEL)],
        out_shape=[
            jax.ShapeDtypeStruct((T, CONV_CH), F32),
            jax.ShapeDtypeStruct((256, T), BF16),
            jax.ShapeDtypeStruct((T, 256), BF16),
            jax.ShapeDtypeStruct((DIFF_HEADS * V_AUG, T), BF16),
            jax.ShapeDtypeStruct((512, T), BF16),
            jax.ShapeDtypeStruct((T, 128), BF16),
            jax.ShapeDtypeStruct((GQA_KV_HEADS * V_AUG, T), BF16),
            jax.ShapeDtypeStruct((T, 3 * D_MODEL), BF16),
        ],
        compiler_params=_params(1),
        name="inproj",
    )(x, gpre, w_in, cd, sd, cg, sg, gqn, gkn, bdq, bdk)


def _conv_kernel(u_ref, w_ref, b_ref, lng_ref, lnb_ref, o_ref, pad_ref):
    zeros = jnp.zeros((CONV_PAD, CONV_CH), F32)
    pad_ref[0:CONV_PAD, :] = zeros
    pad_ref[CONV_PAD + SEQ:CONV_PAD + SEQ + CONV_PAD, :] = zeros
    pad_ref[CONV_PAD:CONV_PAD + SEQ, :] = u_ref[...]
    w = w_ref[...]
    half = CONV_WIDTH // 2
    for c in range(SEQ // CONV_ROWS):
        base = c * CONV_ROWS + CONV_PAD - half
        acc = jnp.broadcast_to(b_ref[...], (CONV_ROWS, CONV_CH))
        for k in range(CONV_WIDTH):
            acc = acc + pad_ref[base + k:base + k + CONV_ROWS, :] * w[k:k + 1, :]
        mu = jnp.mean(acc, axis=-1, keepdims=True)
        d = acc - mu
        var = jnp.mean(d * d, axis=-1, keepdims=True)
        y = d * lax.rsqrt(var + EPS) * lng_ref[...] + lnb_ref[...]
        o_ref[c * CONV_ROWS:(c + 1) * CONV_ROWS, :] = (y * jax.nn.sigmoid(y)).astype(BF16)


def _conv_module(u, w, b, lng, lnb):
    T = u.shape[0]
    return pl.pallas_call(
        _conv_kernel,
        grid=(T // SEQ,),
        in_specs=[pl.BlockSpec((SEQ, CONV_CH), lambda i: (i, 0)),
                  _const_spec((CONV_WIDTH, CONV_CH)), _const_spec((1, CONV_CH)),
                  _const_spec((1, CONV_CH)), _const_spec((1, CONV_CH))],
        out_specs=pl.BlockSpec((SEQ, CONV_CH), lambda i: (i, 0)),
        out_shape=jax.ShapeDtypeStruct((T, CONV_CH), BF16),
        scratch_shapes=[pltpu.VMEM((SEQ + 2 * CONV_PAD, CONV_CH), F32)],
        compiler_params=_params(1),
        name="conv_module",
    )(u, w, b, lng, lnb)


def _attend_t(k, wq, vT_aug):
    s = jnp.dot(k, wq, preferred_element_type=F32)
    p = jnp.exp2(s - jnp.max(s, axis=0, keepdims=True)).astype(BF16)
    o = jnp.dot(vT_aug, p, preferred_element_type=F32)
    return o[:64] * (1.0 / o[64:65])


def _place_rows(block, lo, total):
    rows, n = block.shape
    parts = []
    if lo > 0:
        parts.append(jnp.zeros((lo, n), block.dtype))
    parts.append(block)
    if total - lo - rows > 0:
        parts.append(jnp.zeros((total - lo - rows, n), block.dtype))
    return jnp.concatenate(parts, axis=0) if len(parts) > 1 else block


def _attn_kernel(lam_init, lam_ref, subg_ref, dqT_ref, dk_ref, dvT_ref, gqT_ref, gk_ref, gvT_ref,
                 ob_ref, oc_ref):
    lp = lam_ref[...]
    lam = (jnp.exp(jnp.sum(lp[0:1] * lp[1:2], keepdims=True))
           - jnp.exp(jnp.sum(lp[2:3] * lp[3:4], keepdims=True)) + lam_init)
    dk = dk_ref[...]
    heads = []
    for h in range(DIFF_HEADS):
        r0 = h * 2 * DIFF_QK_DIM
        w1 = _place_rows(dqT_ref[r0:r0 + DIFF_QK_DIM, :], r0, 2 * DIFF_HEADS * DIFF_QK_DIM)
        w2 = _place_rows(dqT_ref[r0 + DIFF_QK_DIM:r0 + 2 * DIFF_QK_DIM, :], r0 + DIFF_QK_DIM,
                         2 * DIFF_HEADS * DIFF_QK_DIM)
        vT = dvT_ref[h * V_AUG:(h + 1) * V_AUG, :]
        o = _attend_t(dk, w1, vT) - lam * _attend_t(dk, w2, vT)
        ms = jnp.mean(o * o, axis=0, keepdims=True)
        heads.append(o * lax.rsqrt(ms + EPS) * subg_ref[...] * (1.0 - lam_init))
    ob_ref[...] = jnp.concatenate(heads, axis=0).T.astype(BF16)

    gk = gk_ref[...]
    heads = []
    per_group = GQA_HEADS // GQA_KV_HEADS
    for h in range(GQA_HEADS):
        g = h // per_group
        wq = _place_rows(gqT_ref[h * GQA_HEAD_DIM:(h + 1) * GQA_HEAD_DIM, :], g * GQA_HEAD_DIM,
                         GQA_KV_HEADS * GQA_HEAD_DIM)
        heads.append(_attend_t(gk, wq, gvT_ref[g * V_AUG:(g + 1) * V_AUG, :]))
    oc_ref[...] = jnp.concatenate(heads, axis=0).T.astype(BF16)


def _attention(lam_init, lam_p, subg, dqT, dk, dvT, gqT, gk, gvT):
    T = dk.shape[0]
    nq = SEQ // TQ
    qcol = lambda h: pl.BlockSpec((h, TQ), lambda b, j: (0, b * nq + j))
    krow = lambda w: pl.BlockSpec((SEQ, w), lambda b, j: (b, 0))
    vcol = lambda h: pl.BlockSpec((h, SEQ), lambda b, j: (0, b))
    orow = lambda w: pl.BlockSpec((TQ, w), lambda b, j: (b * nq + j, 0))
    return pl.pallas_call(
        functools.partial(_attn_kernel, lam_init),
        grid=(T // SEQ, nq),
        in_specs=[_const_spec((4, DIFF_QK_DIM)), _const_spec((DIFF_V_DIM, 1)),
                  qcol(256), krow(256), vcol(DIFF_HEADS * V_AUG),
                  qcol(512), krow(128), vcol(GQA_KV_HEADS * V_AUG)],
        out_specs=[orow(256), orow(512)],
        out_shape=[jax.ShapeDtypeStruct((T, 256), BF16), jax.ShapeDtypeStruct((T, 512), BF16)],
        compiler_params=_params(2),
        name="attention",
    )(lam_p, subg, dqT, dk, dvT, gqT, gk, gvT)


def _merge_kernel(x_ref, ua_ref, ob_ref, oc_ref, gates_ref, wc_ref, wd_ref, wg_ref, wo_ref,
                  gpost_ref, o_ref):
    def gate(j):
        return gates_ref[:, j * D_MODEL:(j + 1) * D_MODEL].astype(F32)

    merged = gate(0) * jnp.dot(ua_ref[...], wc_ref[...], preferred_element_type=F32)
    merged += gate(1) * jnp.dot(ob_ref[...], wd_ref[...], preferred_element_type=F32)
    merged += gate(2) * jnp.dot(oc_ref[...], wg_ref[...], preferred_element_type=F32)
    y = jnp.dot(merged.astype(BF16), wo_ref[...], preferred_element_type=F32)
    o_ref[...] = x_ref[...] + _rms_rows(y, gpost_ref[...])


def _merge(x, ua, ob, oc, gates, wc, wd, wg, wo, gpost):
    T = x.shape[0]
    tm = TM_MERGE
    row = lambda w: pl.BlockSpec((tm, w), lambda i: (i, 0))
    return pl.pallas_call(
        _merge_kernel,
        grid=(T // tm,),
        in_specs=[row(D_MODEL), row(256), row(256), row(512), row(3 * D_MODEL),
                  _const_spec((256, D_MODEL)), _const_spec((256, D_MODEL)),
                  _const_spec((512, D_MODEL)), _const_spec((D_MODEL, D_MODEL)),
                  _const_spec((1, D_MODEL))],
        out_specs=row(D_MODEL),
        out_shape=jax.ShapeDtypeStruct((T, D_MODEL), F32),
        compiler_params=_params(1),
        name="merge",
    )(x, ua, ob, oc, gates, wc, wd, wg, wo, gpost)


def _ffn_kernel(x_ref, xp_ref, xn_ref, p_ref, gpre_ref, wup_ref, dww_ref, dwb_ref, wdown_ref,
                gpost_ref, wple_ref, wpg_ref, o_ref, acc_ref):
    tm = TM_FFN
    i = pl.program_id(0)
    n_seq_tiles = SEQ // tm
    first = (i % n_seq_tiles) == 0
    last = (i % n_seq_tiles) == n_seq_tiles - 1
    x = x_ref[...]
    gpre = gpre_ref[...]
    hp = jnp.where(first, 0.0, _rms_rows(xp_ref[...], gpre))
    hn = jnp.where(last, 0.0, _rms_rows(xn_ref[...], gpre))
    h2 = jnp.concatenate([hp, _rms_rows(x, gpre), hn], axis=0).astype(BF16)

    def conv3(up, c0):
        w = dww_ref[:, c0:c0 + FFN_CHUNK]
        return (up[HALO - 1:HALO - 1 + tm] * w[0:1] + up[HALO:HALO + tm] * w[1:2]
                + up[HALO + 1:HALO + 1 + tm] * w[2:3] + dwb_ref[:, c0:c0 + FFN_CHUNK])

    for c in range(FFN_DIM // FFN_CHUNK):
        cg = c * FFN_CHUNK
        cv = FFN_DIM + c * FFN_CHUNK
        f_gate = conv3(jnp.dot(h2, wup_ref[:, cg:cg + FFN_CHUNK], preferred_element_type=F32), cg)
        f_val = conv3(jnp.dot(h2, wup_ref[:, cv:cv + FFN_CHUNK], preferred_element_type=F32), cv)
        act = (jax.nn.gelu(f_gate, approximate=True) * f_val).astype(BF16)
        part = jnp.dot(act, wdown_ref[cg:cg + FFN_CHUNK, :], preferred_element_type=F32)
        if c == 0:
            acc_ref[...] = part
        else:
            acc_ref[...] += part

    x2 = x + _rms_rows(acc_ref[...], gpost_ref[...])
    gate = jax.nn.sigmoid(jnp.dot(x2.astype(BF16), wpg_ref[...], preferred_element_type=F32))
    emb = jnp.dot(p_ref[...].astype(BF16), wple_ref[...], preferred_element_type=F32)
    o_ref[...] = x2 + gate * emb


def _ffn(x, p, gpre, wup, dww, dwb, wdown, gpost, wple, wpg):
    T = x.shape[0]
    tm = TM_FFN
    per = tm // HALO
    n_halo_blocks = T // HALO
    row = lambda w: pl.BlockSpec((tm, w), lambda i: (i, 0))
    prev = pl.BlockSpec((HALO, D_MODEL), lambda i: (jnp.maximum(i * per - 1, 0), 0))
    nxt = pl.BlockSpec((HALO, D_MODEL),
                       lambda i: (jnp.minimum((i + 1) * per, n_halo_blocks - 1), 0))
    return pl.pallas_call(
        _ffn_kernel,
        grid=(T // tm,),
        in_specs=[row(D_MODEL), prev, nxt, row(PLE_DIM), _const_spec((1, D_MODEL)),
                  _const_spec((D_MODEL, 2 * FFN_DIM)), _const_spec((3, 2 * FFN_DIM)),
                  _const_spec((1, 2 * FFN_DIM)), _const_spec((FFN_DIM, D_MODEL)),
                  _const_spec((1, D_MODEL)), _const_spec((PLE_DIM, D_MODEL)),
                  _const_spec((D_MODEL, D_MODEL))],
        out_specs=row(D_MODEL),
        out_shape=jax.ShapeDtypeStruct((T, D_MODEL), F32),
        scratch_shapes=[pltpu.VMEM((tm, D_MODEL), F32)],
        compiler_params=_params(1),
        name="ffn",
    )(x, x, x, p, gpre, wup, dww, dwb, wdown, gpost, wple, wpg)


def _inv_freq(dim):
    return 1.0 / (ROPE_THETA ** (jnp.arange(0, dim, 2, dtype=F32) / dim))


def _rope_tables():
    t = jnp.arange(SEQ, dtype=F32)
    ang_1d = t[:, None] * _inv_freq(DIFF_QK_DIM)[None, :]
    rows = SEQ // GRID_W
    row = jnp.repeat(jnp.arange(rows, dtype=F32), GRID_W)
    col = jnp.tile(jnp.arange(GRID_W, dtype=F32), rows)
    fr = _inv_freq(GQA_HEAD_DIM // 2)
    ang_2d = jnp.concatenate([row[:, None] * fr[None, :], col[:, None] * fr[None, :]], axis=-1)

    def lanes(ang):
        per_head = jnp.repeat(ang, 2, axis=1)
        reps = LANES // per_head.shape[1]
        cos = jnp.tile(jnp.cos(per_head), (1, reps))
        sign = jnp.tile(jnp.array([-1.0, 1.0], F32), LANES // 2)[None, :]
        sin = jnp.tile(jnp.sin(per_head), (1, reps)) * sign
        return cos, sin

    cd, sd = lanes(ang_1d)
    cg, sg = lanes(ang_2d)
    return cd, sd, cg, sg


def _block_diag_mean(n, group):
    idx = jnp.arange(n) // group
    return jnp.where(idx[:, None] == idx[None, :], 1.0 / group, 0.0).astype(BF16)


def kernel(x, p, norm_mix_pre, norm_mix_post, w_in, conv_dw_w, conv_dw_b, conv_ln_g, conv_ln_b, w_conv_out, diff_lambda, diff_subln_g, w_diff_out, gqa_q_norm, gqa_k_norm, w_gqa_out, w_out, norm_ffn_pre, norm_ffn_post, w_up, ffn_dw_w, ffn_dw_b, w_down, w_ple, w_ple_gate):
    B, S, D = x.shape
    depth = w_in.shape[0]
    assert (S, D) == (SEQ, D_MODEL)
    T = B * S
    tabs = _rope_tables()
    bdq = _block_diag_mean(GQA_HEADS * GQA_HEAD_DIM, GQA_HEAD_DIM)
    bdk = _block_diag_mean(GQA_KV_HEADS * GQA_HEAD_DIM, GQA_HEAD_DIM)
    xf = x.reshape(T, D)
    pf = p.reshape(depth, T, PLE_DIM)
    bf = lambda a: a.astype(BF16)
    r1 = lambda a: a.reshape(1, -1)
    for i in range(depth):
        lam_init = 0.8 - 0.6 * math.exp(-0.3 * i)
        u, dqT, dk, dvT, gqT, gk, gvT, gates = _inproj(
            xf, r1(norm_mix_pre[i]), bf(w_in[i]), tabs,
            r1(jnp.tile(gqa_q_norm[i], GQA_HEADS)), r1(jnp.tile(gqa_k_norm[i], GQA_KV_HEADS)),
            bdq, bdk)
        ua = _conv_module(u, conv_dw_w[i], r1(conv_dw_b[i]), r1(conv_ln_g[i]), r1(conv_ln_b[i]))
        ob, oc = _attention(lam_init, diff_lambda[i], diff_subln_g[i].reshape(-1, 1),
                            dqT, dk, dvT, gqT, gk, gvT)
        xf = _merge(xf, ua, ob, oc, gates, bf(w_conv_out[i]), bf(w_diff_out[i]),
                    bf(w_gqa_out[i]), bf(w_out[i]), r1(norm_mix_post[i]))
        xf = _ffn(xf, pf[i], r1(norm_ffn_pre[i]), bf(w_up[i]), ffn_dw_w[i], r1(ffn_dw_b[i]),
                  bf(w_down[i]), r1(norm_ffn_post[i]), bf(w_ple[i]), bf(w_ple_gate[i]))
    return xf.reshape(B, S, D)
```

```python
import functools
import math

import jax
import jax.numpy as jnp
from jax import lax
from jax.experimental import pallas as pl
from jax.experimental.pallas import tpu as pltpu

F32 = jnp.float32
BF16 = jnp.bfloat16

D_MODEL = 1024
SEQ = 2048
PLE_DIM = 256
GRID_W = 64
ROPE_THETA = 10000.0
EPS = 1e-6
CONV_CH = 256
CONV_WIDTH = 31
DIFF_HEADS = 4
DIFF_QK_DIM = 32
DIFF_V_DIM = 64
GQA_HEADS = 8
GQA_KV_HEADS = 2
GQA_HEAD_DIM = 64
FFN_DIM = 2816

C_CONV = 0
C_DQ = 512
C_DK = 768
C_DV = 1024
C_GQ = 1280
C_GK = 1792
C_GV = 1920
C_GATES = 2048
IN_COLS = 5120

LANES = 128
SUBLANES = 8
VMEM_LIMIT = 56 * 1024 * 1024

TM_IN = 1024
TQ = 512
KEY_CHUNK = 512
ONES_ROWS = 16
V_AUG = 64 + ONES_ROWS
LOG2E = math.log2(math.e)
TM_MERGE = 512
TM_FFN = 512
FFN_CHUNK = 256
HALO = SUBLANES
CONV_ROWS = 128
CONV_PAD = 16


def _const_spec(shape):
    nd = len(shape)
    return pl.BlockSpec(shape, lambda *_: (0,) * nd, pipeline_mode=pl.Buffered(1))


def _params(n_axes):
    return pltpu.CompilerParams(
        dimension_semantics=("arbitrary",) * n_axes, vmem_limit_bytes=VMEM_LIMIT)


def _rms_rows(x, g):
    ms = jnp.mean(x * x, axis=-1, keepdims=True)
    return x * lax.rsqrt(ms + EPS) * g


def _rope_lanes(v, cos, sin_signed):
    outs = []
    for j in range(v.shape[1] // LANES):
        c = v[:, j * LANES:(j + 1) * LANES]
        nxt = pltpu.roll(c, LANES - 1, 1)
        prv = pltpu.roll(c, 1, 1)
        lane = lax.broadcasted_iota(jnp.int32, c.shape, 1)
        swapped = jnp.where(lane % 2 == 0, nxt, prv)
        outs.append(c * cos + swapped * sin_signed)
    return outs[0] if len(outs) == 1 else jnp.concatenate(outs, axis=1)


def _group_mean_sq(v, bd):
    sq = v * v
    hi = sq.astype(BF16)
    lo = (sq - hi.astype(F32)).astype(BF16)
    return (jnp.dot(hi, bd, preferred_element_type=F32)
            + jnp.dot(lo, bd, preferred_element_type=F32))


def _with_ones_rows(vT, heads):
    ones = jnp.ones((ONES_ROWS, vT.shape[1]), vT.dtype)
    parts = []
    for h in range(heads):
        parts += [vT[h * 64:(h + 1) * 64], ones]
    return jnp.concatenate(parts, axis=0)


def _inproj_kernel(x_ref, gpre_ref, w_ref, cd_ref, sd_ref, cg_ref, sg_ref, gqn_ref, gkn_ref,
                   bdq_ref, bdk_ref,
                   u_ref, dqT_ref, dk_ref, dvT_ref, gqT_ref, gk_ref, gvT_ref, gates_ref):
    h = _rms_rows(x_ref[...], gpre_ref[...]).astype(BF16)

    def glu(a):
        u_ref[...] = a[:, :CONV_CH] * jax.nn.sigmoid(a[:, CONV_CH:])

    def diff_q(a):
        dq = _rope_lanes(a, cd_ref[...], sd_ref[...]) * (DIFF_QK_DIM ** -0.5 * LOG2E)
        dqT_ref[...] = dq.T.astype(BF16)

    def diff_k(a):
        dk_ref[...] = _rope_lanes(a, cd_ref[...], sd_ref[...]).astype(BF16)

    def diff_v(a):
        dvT_ref[...] = _with_ones_rows(a.T.astype(BF16), DIFF_HEADS)

    def gqa_q(a):
        gq = a * lax.rsqrt(_group_mean_sq(a, bdq_ref[...]) + EPS) * gqn_ref[...]
        gq = _rope_lanes(gq, cg_ref[...], sg_ref[...]) * (GQA_HEAD_DIM ** -0.5 * LOG2E)
        gqT_ref[...] = gq.T.astype(BF16)

    def gqa_k(a):
        gk = a * lax.rsqrt(_group_mean_sq(a, bdk_ref[...]) + EPS) * gkn_ref[...]
        gk_ref[...] = _rope_lanes(gk, cg_ref[...], sg_ref[...]).astype(BF16)

    def gqa_v(a):
        gvT_ref[...] = _with_ones_rows(a.T.astype(BF16), GQA_KV_HEADS)

    def gate(j, a):
        gates_ref[:, j * D_MODEL:(j + 1) * D_MODEL] = jax.nn.sigmoid(a).astype(BF16)

    segments = [(C_CONV, C_DQ, glu), (C_DQ, C_DK, diff_q), (C_DK, C_DV, diff_k),
                (C_DV, C_GQ, diff_v), (C_GQ, C_GK, gqa_q), (C_GK, C_GV, gqa_k),
                (C_GV, C_GATES, gqa_v)]
    segments += [(C_GATES + j * D_MODEL, C_GATES + (j + 1) * D_MODEL, functools.partial(gate, j))
                 for j in range(3)]

    def mm(seg):
        return jnp.dot(h, w_ref[:, seg[0]:seg[1]], preferred_element_type=F32)

    nxt = mm(segments[0])
    for j, seg in enumerate(segments):
        cur = nxt
        if j + 1 < len(segments):
            nxt = mm(segments[j + 1])
        seg[2](cur)


def _inproj(x, gpre, w_in, tabs, gqn, gkn, bdq, bdk):
    T = x.shape[0]
    tm = TM_IN
    n_seq_tiles = SEQ // tm
    cd, sd, cg, sg = tabs
    row = lambda w: pl.BlockSpec((tm, w), lambda i: (i, 0))
    col = lambda h: pl.BlockSpec((h, tm), lambda i: (0, i))
    tab = pl.BlockSpec((tm, LANES), lambda i: (i % n_seq_tiles, 0))
    return pl.pallas_call(
        _inproj_kernel,
        grid=(T // tm,),
        in_specs=[row(D_MODEL), _const_spec((1, D_MODEL)), _const_spec((D_MODEL, IN_COLS)),
                  tab, tab, tab, tab,
                  _const_spec((1, 512)), _const_spec((1, 128)),
                  _const_spec((512, 512)), _const_spec((128, 128))],
        out_specs=[row(CONV_CH), col(256), row(256), col(DIFF_HEADS * V_AUG), col(512), row(128),
                   col(GQA_KV_HEADS * V_AUG), row(3 * D_MODEL)],
        out_shape=[
            jax.ShapeDtypeStruct((T, CONV_CH), F32),
            jax.ShapeDtypeStruct((256, T), BF16),
            jax.ShapeDtypeStruct((T, 256), BF16),
            jax.ShapeDtypeStruct((DIFF_HEADS * V_AUG, T), BF16),
            jax.ShapeDtypeStruct((512, T), BF16),
            jax.ShapeDtypeStruct((T, 128), BF16),
            jax.ShapeDtypeStruct((GQA_KV_HEADS * V_AUG, T), BF16),
            jax.ShapeDtypeStruct((T, 3 * D_MODEL), BF16),
        ],
        compiler_params=_params(1),
        name="inproj",
    )(x, gpre, w_in, cd, sd, cg, sg, gqn, gkn, bdq, bdk)


def _conv_kernel(u_ref, w_ref, b_ref, lng_ref, lnb_ref, o_ref, pad_ref):
    zeros = jnp.zeros((CONV_PAD, CONV_CH), F32)
    pad_ref[0:CONV_PAD, :] = zeros
    pad_ref[CONV_PAD + SEQ:CONV_PAD + SEQ + CONV_PAD, :] = zeros
    pad_ref[CONV_PAD:CONV_PAD + SEQ, :] = u_ref[...]
    w = w_ref[...]
    half = CONV_WIDTH // 2
    for c in range(SEQ // CONV_ROWS):
        base = c * CONV_ROWS + CONV_PAD - half
        acc = jnp.broadcast_to(b_ref[...], (CONV_ROWS, CONV_CH))
        for k in range(CONV_WIDTH):
            acc = acc + pad_ref[base + k:base + k + CONV_ROWS, :] * w[k:k + 1, :]
        mu = jnp.mean(acc, axis=-1, keepdims=True)
        d = acc - mu
        var = jnp.mean(d * d, axis=-1, keepdims=True)
        y = d * lax.rsqrt(var + EPS) * lng_ref[...] + lnb_ref[...]
        o_ref[c * CONV_ROWS:(c + 1) * CONV_ROWS, :] = (y * jax.nn.sigmoid(y)).astype(BF16)


def _conv_module(u, w, b, lng, lnb):
    T = u.shape[0]
    return pl.pallas_call(
        _conv_kernel,
        grid=(T // SEQ,),
        in_specs=[pl.BlockSpec((SEQ, CONV_CH), lambda i: (i, 0)),
                  _const_spec((CONV_WIDTH, CONV_CH)), _const_spec((1, CONV_CH)),
                  _const_spec((1, CONV_CH)), _const_spec((1, CONV_CH))],
        out_specs=pl.BlockSpec((SEQ, CONV_CH), lambda i: (i, 0)),
        out_shape=jax.ShapeDtypeStruct((T, CONV_CH), BF16),
        scratch_shapes=[pltpu.VMEM((SEQ + 2 * CONV_PAD, CONV_CH), F32)],
        compiler_params=_params(1),
        name="conv_module",
    )(u, w, b, lng, lnb)


def _attention_passes(passes, o_ref):
    n_chunks = SEQ // KEY_CHUNK
    rows = lambda c: slice(c * KEY_CHUNK, (c + 1) * KEY_CHUNK)
    steps = [(i, c) for i in range(len(passes)) for c in range(n_chunks)]
    wqs = {}

    def scores(step):
        i, c = step
        k_ref, make_wq, _, _ = passes[i]
        if c == 0:
            wqs[i] = make_wq()
        return jnp.dot(k_ref[rows(c), :], wqs[i], preferred_element_type=F32)

    s_next = scores(steps[0])
    m = None
    for j, (i, c) in enumerate(steps):
        s = s_next
        if j + 1 < len(steps):
            s_next = scores(steps[j + 1])
        _, _, vT_ref, v0 = passes[i]
        cm = jnp.max(s, axis=0, keepdims=True)
        m_new = cm if c == 0 else jnp.maximum(m, cm)
        p = jnp.exp2(s - m_new).astype(BF16)
        part = jnp.dot(vT_ref[v0:v0 + V_AUG, rows(c)], p, preferred_element_type=F32)
        if c == 0:
            o_ref[i] = part
        else:
            o_ref[i] = o_ref[i] * jnp.exp2(m - m_new) + part
        m = m_new


def _place_rows(block, lo, total):
    rows, n = block.shape
    parts = []
    if lo > 0:
        parts.append(jnp.zeros((lo, n), block.dtype))
    parts.append(block)
    if total - lo - rows > 0:
        parts.append(jnp.zeros((total - lo - rows, n), block.dtype))
    return jnp.concatenate(parts, axis=0) if len(parts) > 1 else block


def _attn_kernel(lam_init, lam_ref, subg_ref, dqT_ref, dk_ref, dvT_ref, gqT_ref, gk_ref, gvT_ref,
                 ob_ref, oc_ref, o_ref):
    passes = []
    for h in range(DIFF_HEADS):
        for m in range(2):
            r0 = (2 * h + m) * DIFF_QK_DIM
            make_wq = functools.partial(
                lambda r0: _place_rows(dqT_ref[r0:r0 + DIFF_QK_DIM, :], r0,
                                       2 * DIFF_HEADS * DIFF_QK_DIM), r0)
            passes.append((dk_ref, make_wq, dvT_ref, h * V_AUG))
    per_group = GQA_HEADS // GQA_KV_HEADS
    for h in range(GQA_HEADS):
        g = h // per_group
        make_wq = functools.partial(
            lambda h, g: _place_rows(gqT_ref[h * GQA_HEAD_DIM:(h + 1) * GQA_HEAD_DIM, :],
                                     g * GQA_HEAD_DIM, GQA_KV_HEADS * GQA_HEAD_DIM), h, g)
        passes.append((gk_ref, make_wq, gvT_ref, g * V_AUG))
    _attention_passes(passes, o_ref)
    outs = []
    for i in range(len(passes)):
        o = o_ref[i]
        outs.append(o[:64] * (1.0 / o[64:65]))

    lp = lam_ref[...]
    lam = (jnp.exp(jnp.sum(lp[0:1] * lp[1:2], keepdims=True))
           - jnp.exp(jnp.sum(lp[2:3] * lp[3:4], keepdims=True)) + lam_init)
    heads = []
    for h in range(DIFF_HEADS):
        o = outs[2 * h] - lam * outs[2 * h + 1]
        ms = jnp.mean(o * o, axis=0, keepdims=True)
        heads.append(o * lax.rsqrt(ms + EPS) * subg_ref[...] * (1.0 - lam_init))
    ob_ref[...] = jnp.concatenate(heads, axis=0).T.astype(BF16)
    oc_ref[...] = jnp.concatenate(outs[2 * DIFF_HEADS:], axis=0).T.astype(BF16)


def _attention(lam_init, lam_p, subg, dqT, dk, dvT, gqT, gk, gvT):
    T = dk.shape[0]
    nq = SEQ // TQ
    qcol = lambda h: pl.BlockSpec((h, TQ), lambda b, j: (0, b * nq + j))
    krow = lambda w: pl.BlockSpec((SEQ, w), lambda b, j: (b, 0))
    vcol = lambda h: pl.BlockSpec((h, SEQ), lambda b, j: (0, b))
    orow = lambda w: pl.BlockSpec((TQ, w), lambda b, j: (b * nq + j, 0))
    return pl.pallas_call(
        functools.partial(_attn_kernel, lam_init),
        grid=(T // SEQ, nq),
        in_specs=[_const_spec((4, DIFF_QK_DIM)), _const_spec((DIFF_V_DIM, 1)),
                  qcol(256), krow(256), vcol(DIFF_HEADS * V_AUG),
                  qcol(512), krow(128), vcol(GQA_KV_HEADS * V_AUG)],
        out_specs=[orow(256), orow(512)],
        out_shape=[jax.ShapeDtypeStruct((T, 256), BF16), jax.ShapeDtypeStruct((T, 512), BF16)],
        scratch_shapes=[pltpu.VMEM((2 * DIFF_HEADS + GQA_HEADS, V_AUG, TQ), F32)],
        compiler_params=_params(2),
        name="attention",
    )(lam_p, subg, dqT, dk, dvT, gqT, gk, gvT)


def _merge_kernel(x_ref, ua_ref, ob_ref, oc_ref, gates_ref, wc_ref, wd_ref, wg_ref, wo_ref,
                  gpost_ref, o_ref):
    def gate(j):
        return gates_ref[:, j * D_MODEL:(j + 1) * D_MODEL].astype(F32)

    merged = gate(0) * jnp.dot(ua_ref[...], wc_ref[...], preferred_element_type=F32)
    merged += gate(1) * jnp.dot(ob_ref[...], wd_ref[...], preferred_element_type=F32)
    merged += gate(2) * jnp.dot(oc_ref[...], wg_ref[...], preferred_element_type=F32)
    y = jnp.dot(merged.astype(BF16), wo_ref[...], preferred_element_type=F32)
    o_ref[...] = x_ref[...] + _rms_rows(y, gpost_ref[...])


def _merge(x, ua, ob, oc, gates, wc, wd, wg, wo, gpost):
    T = x.shape[0]
    tm = TM_MERGE
    row = lambda w: pl.BlockSpec((tm, w), lambda i: (i, 0))
    return pl.pallas_call(
        _merge_kernel,
        grid=(T // tm,),
        in_specs=[row(D_MODEL), row(256), row(256), row(512), row(3 * D_MODEL),
                  _const_spec((256, D_MODEL)), _const_spec((256, D_MODEL)),
                  _const_spec((512, D_MODEL)), _const_spec((D_MODEL, D_MODEL)),
                  _const_spec((1, D_MODEL))],
        out_specs=row(D_MODEL),
        out_shape=jax.ShapeDtypeStruct((T, D_MODEL), F32),
        compiler_params=_params(1),
        name="merge",
    )(x, ua, ob, oc, gates, wc, wd, wg, wo, gpost)


def _ffn_kernel(x_ref, xp_ref, xn_ref, p_ref, gpre_ref, wup_ref, dww_ref, dwb_ref, wdown_ref,
                gpost_ref, wple_ref, wpg_ref, o_ref, acc_ref):
    tm = TM_FFN
    i = pl.program_id(0)
    n_seq_tiles = SEQ // tm
    first = (i % n_seq_tiles) == 0
    last = (i % n_seq_tiles) == n_seq_tiles - 1
    x = x_ref[...]
    gpre = gpre_ref[...]
    hp = jnp.where(first, 0.0, _rms_rows(xp_ref[...], gpre))
    hn = jnp.where(last, 0.0, _rms_rows(xn_ref[...], gpre))
    h2 = jnp.concatenate([hp, _rms_rows(x, gpre), hn], axis=0).astype(BF16)

    def up_proj(c):
        cols = slice(c * 2 * FFN_CHUNK, (c + 1) * 2 * FFN_CHUNK)
        return jnp.dot(h2, wup_ref[:, cols], preferred_element_type=F32)

    n_chunks = FFN_DIM // FFN_CHUNK
    up_next = up_proj(0)
    for c in range(n_chunks):
        up = up_next
        if c + 1 < n_chunks:
            up_next = up_proj(c + 1)
        cols = slice(c * 2 * FFN_CHUNK, (c + 1) * 2 * FFN_CHUNK)
        w = dww_ref[:, cols]
        n_rows = up.shape[0]
        prv = pltpu.roll(up, 1, 0)[HALO:HALO + tm]
        nxt = pltpu.roll(up, n_rows - 1, 0)[HALO:HALO + tm]
        f = prv * w[0:1] + up[HALO:HALO + tm] * w[1:2] + nxt * w[2:3] + dwb_ref[:, cols]
        act = (jax.nn.gelu(f[:, :FFN_CHUNK], approximate=True) * f[:, FFN_CHUNK:]).astype(BF16)
        part = jnp.dot(act, wdown_ref[c * FFN_CHUNK:(c + 1) * FFN_CHUNK, :],
                       preferred_element_type=F32)
        if c == 0:
            acc_ref[...] = part
        else:
            acc_ref[...] += part

    x2 = x + _rms_rows(acc_ref[...], gpost_ref[...])
    gate = jax.nn.sigmoid(jnp.dot(x2.astype(BF16), wpg_ref[...], preferred_element_type=F32))
    emb = jnp.dot(p_ref[...].astype(BF16), wple_ref[...], preferred_element_type=F32)
    o_ref[...] = x2 + gate * emb


def _ffn(x, p, gpre, wup, dww, dwb, wdown, gpost, wple, wpg):
    T = x.shape[0]
    tm = TM_FFN
    per = tm // HALO
    n_halo_blocks = T // HALO
    row = lambda w: pl.BlockSpec((tm, w), lambda i: (i, 0))
    prev = pl.BlockSpec((HALO, D_MODEL), lambda i: (jnp.maximum(i * per - 1, 0), 0))
    nxt = pl.BlockSpec((HALO, D_MODEL),
                       lambda i: (jnp.minimum((i + 1) * per, n_halo_blocks - 1), 0))
    return pl.pallas_call(
        _ffn_kernel,
        grid=(T // tm,),
        in_specs=[row(D_MODEL), prev, nxt, row(PLE_DIM), _const_spec((1, D_MODEL)),
                  _const_spec((D_MODEL, 2 * FFN_DIM)), _const_spec((3, 2 * FFN_DIM)),
                  _const_spec((1, 2 * FFN_DIM)), _const_spec((FFN_DIM, D_MODEL)),
                  _const_spec((1, D_MODEL)), _const_spec((PLE_DIM, D_MODEL)),
                  _const_spec((D_MODEL, D_MODEL))],
        out_specs=row(D_MODEL),
        out_shape=jax.ShapeDtypeStruct((T, D_MODEL), F32),
        scratch_shapes=[pltpu.VMEM((tm, D_MODEL), F32)],
        compiler_params=_params(1),
        name="ffn",
    )(x, x, x, p, gpre, wup, dww, dwb, wdown, gpost, wple, wpg)


def _inv_freq(dim):
    return 1.0 / (ROPE_THETA ** (jnp.arange(0, dim, 2, dtype=F32) / dim))


def _rope_tables():
    t = jnp.arange(SEQ, dtype=F32)
    ang_1d = t[:, None] * _inv_freq(DIFF_QK_DIM)[None, :]
    rows = SEQ // GRID_W
    row = jnp.repeat(jnp.arange(rows, dtype=F32), GRID_W)
    col = jnp.tile(jnp.arange(GRID_W, dtype=F32), rows)
    fr = _inv_freq(GQA_HEAD_DIM // 2)
    ang_2d = jnp.concatenate([row[:, None] * fr[None, :], col[:, None] * fr[None, :]], axis=-1)

    def lanes(ang):
        per_head = jnp.repeat(ang, 2, axis=1)
        reps = LANES // per_head.shape[1]
        cos = jnp.tile(jnp.cos(per_head), (1, reps))
        sign = jnp.tile(jnp.array([-1.0, 1.0], F32), LANES // 2)[None, :]
        sin = jnp.tile(jnp.sin(per_head), (1, reps)) * sign
        return cos, sin

    cd, sd = lanes(ang_1d)
    cg, sg = lanes(ang_2d)
    return cd, sd, cg, sg


def _chunk_gate_value(a):
    rows = a.shape[0]
    a = a.reshape(rows, 2, FFN_DIM // FFN_CHUNK, FFN_CHUNK)
    return jnp.swapaxes(a, 1, 2).reshape(rows, 2 * FFN_DIM)


def _block_diag_mean(n, group):
    idx = jnp.arange(n) // group
    return jnp.where(idx[:, None] == idx[None, :], 1.0 / group, 0.0).astype(BF16)


def kernel(x, p, norm_mix_pre, norm_mix_post, w_in, conv_dw_w, conv_dw_b, conv_ln_g, conv_ln_b, w_conv_out, diff_lambda, diff_subln_g, w_diff_out, gqa_q_norm, gqa_k_norm, w_gqa_out, w_out, norm_ffn_pre, norm_ffn_post, w_up, ffn_dw_w, ffn_dw_b, w_down, w_ple, w_ple_gate):
    B, S, D = x.shape
    depth = w_in.shape[0]
    assert (S, D) == (SEQ, D_MODEL)
    T = B * S
    tabs = _rope_tables()
    bdq = _block_diag_mean(GQA_HEADS * GQA_HEAD_DIM, GQA_HEAD_DIM)
    bdk = _block_diag_mean(GQA_KV_HEADS * GQA_HEAD_DIM, GQA_HEAD_DIM)
    xf = x.reshape(T, D)
    pf = p.reshape(depth, T, PLE_DIM)
    bf = lambda a: a.astype(BF16)
    r1 = lambda a: a.reshape(1, -1)
    for i in range(depth):
        lam_init = 0.8 - 0.6 * math.exp(-0.3 * i)
        u, dqT, dk, dvT, gqT, gk, gvT, gates = _inproj(
            xf, r1(norm_mix_pre[i]), bf(w_in[i]), tabs,
            r1(jnp.tile(gqa_q_norm[i], GQA_HEADS)), r1(jnp.tile(gqa_k_norm[i], GQA_KV_HEADS)),
            bdq, bdk)
        ua = _conv_module(u, conv_dw_w[i], r1(conv_dw_b[i]), r1(conv_ln_g[i]), r1(conv_ln_b[i]))
        ob, oc = _attention(lam_init, diff_lambda[i], diff_subln_g[i].reshape(-1, 1),
                            dqT, dk, dvT, gqT, gk, gvT)
        xf = _merge(xf, ua, ob, oc, gates, bf(w_conv_out[i]), bf(w_diff_out[i]),
                    bf(w_gqa_out[i]), bf(w_out[i]), r1(norm_mix_post[i]))
        xf = _ffn(xf, pf[i], r1(norm_ffn_pre[i]), bf(_chunk_gate_value(w_up[i])),
                  _chunk_gate_value(ffn_dw_w[i]), _chunk_gate_value(r1(ffn_dw_b[i])),
                  bf(w_down[i]), r1(norm_ffn_post[i]), bf(w_ple[i]), bf(w_ple_gate[i]))
    return xf.reshape(B, S, D)
```

```python
import functools
import math

import jax
import jax.numpy as jnp
from jax import lax
from jax.experimental import pallas as pl
from jax.experimental.pallas import tpu as pltpu

F32 = jnp.float32
BF16 = jnp.bfloat16

D_MODEL = 1024
SEQ = 2048
PLE_DIM = 256
GRID_W = 64
ROPE_THETA = 10000.0
EPS = 1e-6
CONV_CH = 256
CONV_WIDTH = 31
DIFF_HEADS = 4
DIFF_QK_DIM = 32
DIFF_V_DIM = 64
GQA_HEADS = 8
GQA_KV_HEADS = 2
GQA_HEAD_DIM = 64
FFN_DIM = 2816

C_CONV = 0
C_DQ = 512
C_DK = 768
C_DV = 1024
C_GQ = 1280
C_GK = 1792
C_GV = 1920
C_GATES = 2048
IN_COLS = 5120

LANES = 128
SUBLANES = 8
VMEM_LIMIT = 56 * 1024 * 1024

TM_IN = 512
TQ = 512
KEY_CHUNK = 512
ONES_ROWS = 16
V_AUG = 64 + ONES_ROWS
LOG2E = math.log2(math.e)
TM_MERGE = 512
TM_FFN = 512
FFN_CHUNK = 256
HALO = SUBLANES
CONV_ROWS = 32
CONV_TAPS = 8
CONV_PAD = 16


def _const_spec(shape):
    nd = len(shape)
    return pl.BlockSpec(shape, lambda *_: (0,) * nd, pipeline_mode=pl.Buffered(1))


def _params(n_axes):
    return pltpu.CompilerParams(
        dimension_semantics=("arbitrary",) * n_axes, vmem_limit_bytes=VMEM_LIMIT)


def _rms_rows(x, g):
    ms = jnp.mean(x * x, axis=-1, keepdims=True)
    return x * lax.rsqrt(ms + EPS) * g


def _rope_lanes(v, cos, sin_signed):
    outs = []
    for j in range(v.shape[1] // LANES):
        c = v[:, j * LANES:(j + 1) * LANES]
        nxt = pltpu.roll(c, LANES - 1, 1)
        prv = pltpu.roll(c, 1, 1)
        lane = lax.broadcasted_iota(jnp.int32, c.shape, 1)
        swapped = jnp.where(lane % 2 == 0, nxt, prv)
        outs.append(c * cos + swapped * sin_signed)
    return outs[0] if len(outs) == 1 else jnp.concatenate(outs, axis=1)


def _group_mean_sq(v, bd):
    sq = v * v
    hi = sq.astype(BF16)
    lo = (sq - hi.astype(F32)).astype(BF16)
    return (jnp.dot(hi, bd, preferred_element_type=F32)
            + jnp.dot(lo, bd, preferred_element_type=F32))


def _with_ones_rows(vT, heads):
    ones = jnp.ones((ONES_ROWS, vT.shape[1]), vT.dtype)
    parts = []
    for h in range(heads):
        parts += [vT[h * 64:(h + 1) * 64], ones]
    return jnp.concatenate(parts, axis=0)


def _inproj_kernel(x_ref, gpre_ref, w_ref, cd_ref, sd_ref, cg_ref, sg_ref, gqn_ref, gkn_ref,
                   bdq_ref, bdk_ref,
                   u_ref, dqT_ref, dk_ref, dvT_ref, gqT_ref, gk_ref, gvT_ref, gates_ref):
    h = _rms_rows(x_ref[...], gpre_ref[...]).astype(BF16)

    def glu(a):
        u_ref[...] = a[:, :CONV_CH] * jax.nn.sigmoid(a[:, CONV_CH:])

    def diff_q(a):
        dq = _rope_lanes(a, cd_ref[...], sd_ref[...]) * (DIFF_QK_DIM ** -0.5 * LOG2E)
        dqT_ref[...] = dq.T.astype(BF16)

    def diff_k(a):
        dk_ref[...] = _rope_lanes(a, cd_ref[...], sd_ref[...]).astype(BF16)

    def diff_v(a):
        dvT_ref[...] = _with_ones_rows(a.T.astype(BF16), DIFF_HEADS)

    def gqa_q(a):
        gq = a * lax.rsqrt(_group_mean_sq(a, bdq_ref[...]) + EPS) * gqn_ref[...]
        gq = _rope_lanes(gq, cg_ref[...], sg_ref[...]) * (GQA_HEAD_DIM ** -0.5 * LOG2E)
        gqT_ref[...] = gq.T.astype(BF16)

    def gqa_k(a):
        gk = a * lax.rsqrt(_group_mean_sq(a, bdk_ref[...]) + EPS) * gkn_ref[...]
        gk_ref[...] = _rope_lanes(gk, cg_ref[...], sg_ref[...]).astype(BF16)

    def gqa_v(a):
        gvT_ref[...] = _with_ones_rows(a.T.astype(BF16), GQA_KV_HEADS)

    def gate(j, a):
        gates_ref[:, j * D_MODEL:(j + 1) * D_MODEL] = jax.nn.sigmoid(a).astype(BF16)

    segments = [(C_CONV, C_DQ, glu), (C_DQ, C_DK, diff_q), (C_DK, C_DV, diff_k),
                (C_DV, C_GQ, diff_v), (C_GQ, C_GK, gqa_q), (C_GK, C_GV, gqa_k),
                (C_GV, C_GATES, gqa_v)]
    segments += [(C_GATES + j * D_MODEL, C_GATES + (j + 1) * D_MODEL, functools.partial(gate, j))
                 for j in range(3)]

    def mm(seg):
        return jnp.dot(h, w_ref[:, seg[0]:seg[1]].astype(BF16), preferred_element_type=F32)

    nxt = mm(segments[0])
    for j, seg in enumerate(segments):
        cur = nxt
        if j + 1 < len(segments):
            nxt = mm(segments[j + 1])
        seg[2](cur)


def _inproj(x, gpre, w_in, tabs, gqn, gkn, bdq, bdk):
    T = x.shape[0]
    tm = TM_IN
    n_seq_tiles = SEQ // tm
    cd, sd, cg, sg = tabs
    row = lambda w: pl.BlockSpec((tm, w), lambda i: (i, 0))
    col = lambda h: pl.BlockSpec((h, tm), lambda i: (0, i))
    tab = pl.BlockSpec((tm, LANES), lambda i: (i % n_seq_tiles, 0))
    return pl.pallas_call(
        _inproj_kernel,
        grid=(T // tm,),
        in_specs=[row(D_MODEL), _const_spec((1, D_MODEL)), _const_spec((D_MODEL, IN_COLS)),
                  tab, tab, tab, tab,
                  _const_spec((1, 512)), _const_spec((1, 128)),
                  _const_spec((512, 512)), _const_spec((128, 128))],
        out_specs=[row(CONV_CH), col(256), row(256), col(DIFF_HEADS * V_AUG), col(512), row(128),
                   col(GQA_KV_HEADS * V_AUG), row(3 * D_MODEL)],
        out_shape=[
            jax.ShapeDtypeStruct((T, CONV_CH), F32),
            jax.ShapeDtypeStruct((256, T), BF16),
            jax.ShapeDtypeStruct((T, 256), BF16),
            jax.ShapeDtypeStruct((DIFF_HEADS * V_AUG, T), BF16),
            jax.ShapeDtypeStruct((512, T), BF16),
            jax.ShapeDtypeStruct((T, 128), BF16),
            jax.ShapeDtypeStruct((GQA_KV_HEADS * V_AUG, T), BF16),
            jax.ShapeDtypeStruct((T, 3 * D_MODEL), BF16),
        ],
        compiler_params=_params(1),
        name="inproj",
    )(x, gpre, w_in, cd, sd, cg, sg, gqn, gkn, bdq, bdk)


def _conv_taps(c, g, pad_ref, w_ref, b_ref, lng_ref, lnb_ref, o_ref, acc_ref, anchor):
    base = c * CONV_ROWS + CONV_PAD - CONV_WIDTH // 2
    k0 = g * CONV_TAPS
    n_taps = min(CONV_TAPS, CONV_WIDTH - k0)
    zero = pltpu.bitcast(
        lax.shift_right_logical(pltpu.bitcast(anchor[:, :CONV_CH], jnp.uint32), jnp.uint32(32)),
        F32)
    w = w_ref[k0:k0 + n_taps, :] + zero
    acc = jnp.broadcast_to(b_ref[...], (CONV_ROWS, CONV_CH)) if g == 0 else acc_ref[...]
    for k in range(n_taps):
        acc = acc + pad_ref[base + k0 + k:base + k0 + k + CONV_ROWS, :] * w[k:k + 1, :]
    if k0 + n_taps < CONV_WIDTH:
        acc_ref[...] = acc
        return
    mu = jnp.mean(acc, axis=-1, keepdims=True)
    d = acc - mu
    var = jnp.mean(d * d, axis=-1, keepdims=True)
    y = d * lax.rsqrt(var + EPS) * lng_ref[...] + lnb_ref[...]
    o_ref[c * CONV_ROWS:(c + 1) * CONV_ROWS, :] = (y * jax.nn.sigmoid(y)).astype(BF16)


def _attention_passes(passes, o_ref, side_work):
    n_chunks = SEQ // KEY_CHUNK
    rows = lambda c: slice(c * KEY_CHUNK, (c + 1) * KEY_CHUNK)
    steps = [(i, c) for i in range(len(passes)) for c in range(n_chunks)]
    stride = len(steps) // len(side_work)
    wqs = {}

    def scores(step):
        i, c = step
        k_ref, make_wq, _, _ = passes[i]
        if c == 0:
            wqs[i] = make_wq()
        return jnp.dot(k_ref[rows(c), :], wqs[i], preferred_element_type=F32)

    s_next = scores(steps[0])
    m = None
    for j, (i, c) in enumerate(steps):
        s = s_next
        if j + 1 < len(steps):
            s_next = scores(steps[j + 1])
        _, _, vT_ref, v0 = passes[i]
        cm = jnp.max(s, axis=0, keepdims=True)
        m_new = cm if c == 0 else jnp.maximum(m, cm)
        p = jnp.exp2(s - m_new).astype(BF16)
        part = jnp.dot(vT_ref[v0:v0 + V_AUG, rows(c)], p, preferred_element_type=F32)
        if c == 0:
            o_ref[i] = part
        else:
            o_ref[i] = o_ref[i] * jnp.exp2(m - m_new) + part
        m = m_new
        if (j + 1) % stride == 0 and (j + 1) // stride <= len(side_work):
            side_work[(j + 1) // stride - 1](m_new)


def _place_rows(block, lo, total):
    rows, n = block.shape
    parts = []
    if lo > 0:
        parts.append(jnp.zeros((lo, n), block.dtype))
    parts.append(block)
    if total - lo - rows > 0:
        parts.append(jnp.zeros((total - lo - rows, n), block.dtype))
    return jnp.concatenate(parts, axis=0) if len(parts) > 1 else block


def _attn_kernel(lam_init, lam_ref, subg_ref, dqT_ref, dk_ref, dvT_ref, gqT_ref, gk_ref, gvT_ref,
                 u_ref, up_ref, un_ref, cw_ref, cb_ref, lng_ref, lnb_ref,
                 ob_ref, oc_ref, ua_ref, o_ref, pad_ref, acc_ref):
    j = pl.program_id(1)
    pad_ref[0:CONV_PAD, :] = jnp.where(j == 0, 0.0, up_ref[...])
    pad_ref[CONV_PAD:CONV_PAD + TQ, :] = u_ref[...]
    pad_ref[CONV_PAD + TQ:, :] = jnp.where(j == pl.num_programs(1) - 1, 0.0, un_ref[...])
    conv_work = [functools.partial(_conv_taps, c, g, pad_ref, cw_ref, cb_ref, lng_ref, lnb_ref,
                                   ua_ref, acc_ref)
                 for c in range(TQ // CONV_ROWS) for g in range(-(-CONV_WIDTH // CONV_TAPS))]

    passes = []
    for h in range(DIFF_HEADS):
        for m in range(2):
            r0 = (2 * h + m) * DIFF_QK_DIM
            make_wq = functools.partial(
                lambda r0: _place_rows(dqT_ref[r0:r0 + DIFF_QK_DIM, :], r0,
                                       2 * DIFF_HEADS * DIFF_QK_DIM), r0)
            passes.append((dk_ref, make_wq, dvT_ref, h * V_AUG))
    per_group = GQA_HEADS // GQA_KV_HEADS
    for h in range(GQA_HEADS):
        g = h // per_group
        make_wq = functools.partial(
            lambda h, g: _place_rows(gqT_ref[h * GQA_HEAD_DIM:(h + 1) * GQA_HEAD_DIM, :],
                                     g * GQA_HEAD_DIM, GQA_KV_HEADS * GQA_HEAD_DIM), h, g)
        passes.append((gk_ref, make_wq, gvT_ref, g * V_AUG))
    _attention_passes(passes, o_ref, conv_work)
    outs = []
    for i in range(len(passes)):
        o = o_ref[i]
        outs.append(o[:64] * (1.0 / o[64:65]))

    lp = lam_ref[...]
    lam = (jnp.exp(jnp.sum(lp[0:1] * lp[1:2], keepdims=True))
           - jnp.exp(jnp.sum(lp[2:3] * lp[3:4], keepdims=True)) + lam_init)
    heads = []
    for h in range(DIFF_HEADS):
        o = outs[2 * h] - lam * outs[2 * h + 1]
        ms = jnp.mean(o * o, axis=0, keepdims=True)
        heads.append(o * lax.rsqrt(ms + EPS) * subg_ref[...] * (1.0 - lam_init))
    ob_ref[...] = jnp.concatenate(heads, axis=0).T.astype(BF16)
    oc_ref[...] = jnp.concatenate(outs[2 * DIFF_HEADS:], axis=0).T.astype(BF16)


def _attention(lam_init, lam_p, subg, dqT, dk, dvT, gqT, gk, gvT, u, cw, cb, lng, lnb):
    T = dk.shape[0]
    nq = SEQ // TQ
    per = TQ // CONV_PAD
    n_halo_blocks = T // CONV_PAD
    qcol = lambda h: pl.BlockSpec((h, TQ), lambda b, j: (0, b * nq + j))
    krow = lambda w: pl.BlockSpec((SEQ, w), lambda b, j: (b, 0))
    vcol = lambda h: pl.BlockSpec((h, SEQ), lambda b, j: (0, b))
    orow = lambda w: pl.BlockSpec((TQ, w), lambda b, j: (b * nq + j, 0))
    prev = pl.BlockSpec((CONV_PAD, CONV_CH),
                        lambda b, j: (jnp.maximum((b * nq + j) * per - 1, 0), 0))
    nxt = pl.BlockSpec((CONV_PAD, CONV_CH),
                       lambda b, j: (jnp.minimum((b * nq + j + 1) * per, n_halo_blocks - 1), 0))
    return pl.pallas_call(
        functools.partial(_attn_kernel, lam_init),
        grid=(T // SEQ, nq),
        in_specs=[_const_spec((4, DIFF_QK_DIM)), _const_spec((DIFF_V_DIM, 1)),
                  qcol(256), krow(256), vcol(DIFF_HEADS * V_AUG),
                  qcol(512), krow(128), vcol(GQA_KV_HEADS * V_AUG),
                  orow(CONV_CH), prev, nxt,
                  _const_spec((CONV_WIDTH, CONV_CH)), _const_spec((1, CONV_CH)),
                  _const_spec((1, CONV_CH)), _const_spec((1, CONV_CH))],
        out_specs=[orow(256), orow(512), orow(CONV_CH)],
        out_shape=[jax.ShapeDtypeStruct((T, 256), BF16), jax.ShapeDtypeStruct((T, 512), BF16),
                   jax.ShapeDtypeStruct((T, CONV_CH), BF16)],
        scratch_shapes=[pltpu.VMEM((2 * DIFF_HEADS + GQA_HEADS, V_AUG, TQ), F32),
                        pltpu.VMEM((TQ + 2 * CONV_PAD, CONV_CH), F32),
                        pltpu.VMEM((CONV_ROWS, CONV_CH), F32)],
        compiler_params=_params(2),
        name="attention",
    )(lam_p, subg, dqT, dk, dvT, gqT, gk, gvT, u, u, u, cw, cb, lng, lnb)


def _merge_kernel(x_ref, ua_ref, ob_ref, oc_ref, gates_ref, wc_ref, wd_ref, wg_ref, wo_ref,
                  gpost_ref, o_ref):
    def gate(j):
        return gates_ref[:, j * D_MODEL:(j + 1) * D_MODEL].astype(F32)

    merged = gate(0) * jnp.dot(ua_ref[...], wc_ref[...], preferred_element_type=F32)
    merged += gate(1) * jnp.dot(ob_ref[...], wd_ref[...], preferred_element_type=F32)
    merged += gate(2) * jnp.dot(oc_ref[...], wg_ref[...], preferred_element_type=F32)
    y = jnp.dot(merged.astype(BF16), wo_ref[...], preferred_element_type=F32)
    o_ref[...] = x_ref[...] + _rms_rows(y, gpost_ref[...])


def _merge(x, ua, ob, oc, gates, wc, wd, wg, wo, gpost):
    T = x.shape[0]
    tm = TM_MERGE
    row = lambda w: pl.BlockSpec((tm, w), lambda i: (i, 0))
    return pl.pallas_call(
        _merge_kernel,
        grid=(T // tm,),
        in_specs=[row(D_MODEL), row(256), row(256), row(512), row(3 * D_MODEL),
                  _const_spec((256, D_MODEL)), _const_spec((256, D_MODEL)),
                  _const_spec((512, D_MODEL)), _const_spec((D_MODEL, D_MODEL)),
                  _const_spec((1, D_MODEL))],
        out_specs=row(D_MODEL),
        out_shape=jax.ShapeDtypeStruct((T, D_MODEL), F32),
        compiler_params=_params(1),
        name="merge",
    )(x, ua, ob, oc, gates, wc, wd, wg, wo, gpost)


def _ffn_kernel(x_ref, xp_ref, xn_ref, p_ref, gpre_ref, wup_ref, dww_ref, dwb_ref, wdown_ref,
                gpost_ref, wple_ref, wpg_ref, o_ref, acc_ref):
    tm = TM_FFN
    i = pl.program_id(0)
    n_seq_tiles = SEQ // tm
    first = (i % n_seq_tiles) == 0
    last = (i % n_seq_tiles) == n_seq_tiles - 1
    x = x_ref[...]
    gpre = gpre_ref[...]
    hp = jnp.where(first, 0.0, _rms_rows(xp_ref[...], gpre))
    hn = jnp.where(last, 0.0, _rms_rows(xn_ref[...], gpre))
    h2 = jnp.concatenate([hp, _rms_rows(x, gpre), hn], axis=0).astype(BF16)

    def up_proj(c):
        return tuple(
            jnp.dot(h2, wup_ref[:, c0 + c * FFN_CHUNK:c0 + (c + 1) * FFN_CHUNK],
                    preferred_element_type=F32) for c0 in (0, FFN_DIM))

    def conv3(up, c0):
        cols = slice(c0, c0 + FFN_CHUNK)
        w = dww_ref[:, cols]
        prv = pltpu.roll(up, 1, 0)[HALO:HALO + tm]
        nxt = pltpu.roll(up, up.shape[0] - 1, 0)[HALO:HALO + tm]
        return prv * w[0:1] + up[HALO:HALO + tm] * w[1:2] + nxt * w[2:3] + dwb_ref[:, cols]

    n_chunks = FFN_DIM // FFN_CHUNK
    up_next = up_proj(0)
    for c in range(n_chunks):
        up_g, up_v = up_next
        if c + 1 < n_chunks:
            up_next = up_proj(c + 1)
        f_gate = conv3(up_g, c * FFN_CHUNK)
        f_val = conv3(up_v, FFN_DIM + c * FFN_CHUNK)
        act = (jax.nn.gelu(f_gate, approximate=True) * f_val).astype(BF16)
        part = jnp.dot(act, wdown_ref[c * FFN_CHUNK:(c + 1) * FFN_CHUNK, :],
                       preferred_element_type=F32)
        if c == 0:
            acc_ref[...] = part
        else:
            acc_ref[...] += part

    x2 = x + _rms_rows(acc_ref[...], gpost_ref[...])
    gate = jax.nn.sigmoid(jnp.dot(x2.astype(BF16), wpg_ref[...], preferred_element_type=F32))
    emb = jnp.dot(p_ref[...].astype(BF16), wple_ref[...], preferred_element_type=F32)
    o_ref[...] = x2 + gate * emb


def _ffn(x, p, gpre, wup, dww, dwb, wdown, gpost, wple, wpg):
    T = x.shape[0]
    tm = TM_FFN
    per = tm // HALO
    n_halo_blocks = T // HALO
    row = lambda w: pl.BlockSpec((tm, w), lambda i: (i, 0))
    prev = pl.BlockSpec((HALO, D_MODEL), lambda i: (jnp.maximum(i * per - 1, 0), 0))
    nxt = pl.BlockSpec((HALO, D_MODEL),
                       lambda i: (jnp.minimum((i + 1) * per, n_halo_blocks - 1), 0))
    return pl.pallas_call(
        _ffn_kernel,
        grid=(T // tm,),
        in_specs=[row(D_MODEL), prev, nxt, row(PLE_DIM), _const_spec((1, D_MODEL)),
                  _const_spec((D_MODEL, 2 * FFN_DIM)), _const_spec((3, 2 * FFN_DIM)),
                  _const_spec((1, 2 * FFN_DIM)), _const_spec((FFN_DIM, D_MODEL)),
                  _const_spec((1, D_MODEL)), _const_spec((PLE_DIM, D_MODEL)),
                  _const_spec((D_MODEL, D_MODEL))],
        out_specs=row(D_MODEL),
        out_shape=jax.ShapeDtypeStruct((T, D_MODEL), F32),
        scratch_shapes=[pltpu.VMEM((tm, D_MODEL), F32)],
        compiler_params=_params(1),
        name="ffn",
    )(x, x, x, p, gpre, wup, dww, dwb, wdown, gpost, wple, wpg)


def _inv_freq(dim):
    return 1.0 / (ROPE_THETA ** (jnp.arange(0, dim, 2, dtype=F32) / dim))


def _rope_tables():
    t = jnp.arange(SEQ, dtype=F32)
    ang_1d = t[:, None] * _inv_freq(DIFF_QK_DIM)[None, :]
    rows = SEQ // GRID_W
    row = jnp.repeat(jnp.arange(rows, dtype=F32), GRID_W)
    col = jnp.tile(jnp.arange(GRID_W, dtype=F32), rows)
    fr = _inv_freq(GQA_HEAD_DIM // 2)
    ang_2d = jnp.concatenate([row[:, None] * fr[None, :], col[:, None] * fr[None, :]], axis=-1)

    def lanes(ang):
        per_head = jnp.repeat(ang, 2, axis=1)
        reps = LANES // per_head.shape[1]
        cos = jnp.tile(jnp.cos(per_head), (1, reps))
        sign = jnp.tile(jnp.array([-1.0, 1.0], F32), LANES // 2)[None, :]
        sin = jnp.tile(jnp.sin(per_head), (1, reps)) * sign
        return cos, sin

    cd, sd = lanes(ang_1d)
    cg, sg = lanes(ang_2d)
    return cd, sd, cg, sg


def _block_diag_mean(n, group):
    idx = jnp.arange(n) // group
    return jnp.where(idx[:, None] == idx[None, :], 1.0 / group, 0.0).astype(BF16)


def kernel(x, p, norm_mix_pre, norm_mix_post, w_in, conv_dw_w, conv_dw_b, conv_ln_g, conv_ln_b, w_conv_out, diff_lambda, diff_subln_g, w_diff_out, gqa_q_norm, gqa_k_norm, w_gqa_out, w_out, norm_ffn_pre, norm_ffn_post, w_up, ffn_dw_w, ffn_dw_b, w_down, w_ple, w_ple_gate):
    B, S, D = x.shape
    depth = w_in.shape[0]
    assert (S, D) == (SEQ, D_MODEL)
    T = B * S
    tabs = _rope_tables()
    bdq = _block_diag_mean(GQA_HEADS * GQA_HEAD_DIM, GQA_HEAD_DIM)
    bdk = _block_diag_mean(GQA_KV_HEADS * GQA_HEAD_DIM, GQA_HEAD_DIM)
    xf = x.reshape(T, D)
    pf = p.reshape(depth, T, PLE_DIM)
    bf = lambda a: a.astype(BF16)
    r1 = lambda a: a.reshape(1, -1)
    for i in range(depth):
        lam_init = 0.8 - 0.6 * math.exp(-0.3 * i)
        u, dqT, dk, dvT, gqT, gk, gvT, gates = _inproj(
            xf, r1(norm_mix_pre[i]), w_in[i], tabs,
            r1(jnp.tile(gqa_q_norm[i], GQA_HEADS)), r1(jnp.tile(gqa_k_norm[i], GQA_KV_HEADS)),
            bdq, bdk)
        ob, oc, ua = _attention(lam_init, diff_lambda[i], diff_subln_g[i].reshape(-1, 1),
                                dqT, dk, dvT, gqT, gk, gvT,
                                u, conv_dw_w[i], r1(conv_dw_b[i]), r1(conv_ln_g[i]),
                                r1(conv_ln_b[i]))
        xf = _merge(xf, ua, ob, oc, gates, bf(w_conv_out[i]), bf(w_diff_out[i]),
                    bf(w_gqa_out[i]), bf(w_out[i]), r1(norm_mix_post[i]))
        xf = _ffn(xf, pf[i], r1(norm_ffn_pre[i]), bf(w_up[i]), ffn_dw_w[i], r1(ffn_dw_b[i]),
                  bf(w_down[i]), r1(norm_ffn_post[i]), bf(w_ple[i]), bf(w_ple_gate[i]))
    return xf.reshape(B, S, D)
```

```python
import functools
import math

import jax
import jax.numpy as jnp
from jax import lax
from jax.experimental import pallas as pl
from jax.experimental.pallas import tpu as pltpu

F32 = jnp.float32
BF16 = jnp.bfloat16

D_MODEL = 1024
SEQ = 2048
PLE_DIM = 256
GRID_W = 64
ROPE_THETA = 10000.0
EPS = 1e-6
CONV_CH = 256
CONV_WIDTH = 31
DIFF_HEADS = 4
DIFF_QK_DIM = 32
DIFF_V_DIM = 64
GQA_HEADS = 8
GQA_KV_HEADS = 2
GQA_HEAD_DIM = 64
FFN_DIM = 2816

C_CONV = 0
C_DQ = 512
C_DK = 768
C_DV = 1024
C_GQ = 1280
C_GK = 1792
C_GV = 1920
C_GATES = 2048
IN_COLS = 5120

LANES = 128
SUBLANES = 8
VMEM_LIMIT = 56 * 1024 * 1024

TM_IN = 512
TQ = 512
KEY_CHUNK = 512
ONES_ROWS = 16
V_AUG = 64 + ONES_ROWS
LOG2E = math.log2(math.e)
TM_MERGE = 512
TM_FFN = 512
FFN_CHUNK = 256
HALO = SUBLANES
CONV_ROWS = 32
CONV_TAPS = 8
CONV_PAD = 16


def _const_spec(shape):
    nd = len(shape)
    return pl.BlockSpec(shape, lambda *_: (0,) * nd, pipeline_mode=pl.Buffered(1))


def _layer_spec(layer, shape):
    nd = len(shape)
    return pl.BlockSpec((None,) + tuple(shape), lambda *_: (layer,) + (0,) * nd,
                        pipeline_mode=pl.Buffered(1))


def _params(n_axes):
    return pltpu.CompilerParams(
        dimension_semantics=("arbitrary",) * n_axes, vmem_limit_bytes=VMEM_LIMIT)


def _rms_rows(x, g):
    ms = jnp.mean(x * x, axis=-1, keepdims=True)
    return x * lax.rsqrt(ms + EPS) * g


def _rope_lanes(v, cos, sin_signed):
    outs = []
    for j in range(v.shape[1] // LANES):
        c = v[:, j * LANES:(j + 1) * LANES]
        nxt = pltpu.roll(c, LANES - 1, 1)
        prv = pltpu.roll(c, 1, 1)
        lane = lax.broadcasted_iota(jnp.int32, c.shape, 1)
        swapped = jnp.where(lane % 2 == 0, nxt, prv)
        outs.append(c * cos + swapped * sin_signed)
    return outs[0] if len(outs) == 1 else jnp.concatenate(outs, axis=1)


def _group_mean_sq(v, bd):
    sq = v * v
    hi = sq.astype(BF16)
    lo = (sq - hi.astype(F32)).astype(BF16)
    return (jnp.dot(hi, bd, preferred_element_type=F32)
            + jnp.dot(lo, bd, preferred_element_type=F32))


def _with_ones_rows(vT, heads):
    ones = jnp.ones((ONES_ROWS, vT.shape[1]), vT.dtype)
    parts = []
    for h in range(heads):
        parts += [vT[h * 64:(h + 1) * 64], ones]
    return jnp.concatenate(parts, axis=0)


def _inproj_kernel(x_ref, gpre_ref, w_ref, cd_ref, sd_ref, cg_ref, sg_ref, gqn_ref, gkn_ref,
                   bdq_ref, bdk_ref,
                   u_ref, dqT_ref, dk_ref, dvT_ref, gqT_ref, gk_ref, gvT_ref, gates_ref):
    h = _rms_rows(x_ref[...], gpre_ref[...]).astype(BF16)

    def glu(a):
        u_ref[...] = a[:, :CONV_CH] * jax.nn.sigmoid(a[:, CONV_CH:])

    def diff_q(a):
        dq = _rope_lanes(a, cd_ref[...], sd_ref[...]) * (DIFF_QK_DIM ** -0.5 * LOG2E)
        dqT_ref[...] = dq.T.astype(BF16)

    def diff_k(a):
        dk_ref[...] = _rope_lanes(a, cd_ref[...], sd_ref[...]).astype(BF16)

    def diff_v(a):
        dvT_ref[...] = _with_ones_rows(a.T.astype(BF16), DIFF_HEADS)

    def gqa_q(a):
        gq = a * lax.rsqrt(_group_mean_sq(a, bdq_ref[...]) + EPS) * gqn_ref[...]
        gq = _rope_lanes(gq, cg_ref[...], sg_ref[...]) * (GQA_HEAD_DIM ** -0.5 * LOG2E)
        gqT_ref[...] = gq.T.astype(BF16)

    def gqa_k(a):
        gk = a * lax.rsqrt(_group_mean_sq(a, bdk_ref[...]) + EPS) * gkn_ref[...]
        gk_ref[...] = _rope_lanes(gk, cg_ref[...], sg_ref[...]).astype(BF16)

    def gqa_v(a):
        gvT_ref[...] = _with_ones_rows(a.T.astype(BF16), GQA_KV_HEADS)

    def gate(j, a):
        gates_ref[:, j * D_MODEL:(j + 1) * D_MODEL] = jax.nn.sigmoid(a).astype(BF16)

    segments = [(C_CONV, C_DQ, glu), (C_DQ, C_DK, diff_q), (C_DK, C_DV, diff_k),
                (C_DV, C_GQ, diff_v), (C_GQ, C_GK, gqa_q), (C_GK, C_GV, gqa_k),
                (C_GV, C_GATES, gqa_v)]
    segments += [(C_GATES + j * D_MODEL, C_GATES + (j + 1) * D_MODEL, functools.partial(gate, j))
                 for j in range(3)]

    def mm(seg):
        return jnp.dot(h, w_ref[:, seg[0]:seg[1]].astype(BF16), preferred_element_type=F32)

    nxt = mm(segments[0])
    for j, seg in enumerate(segments):
        cur = nxt
        if j + 1 < len(segments):
            nxt = mm(segments[j + 1])
        seg[2](cur)


def _inproj(layer, x, gpre, w_in, tabs, gqn, gkn, bdq, bdk):
    T = x.shape[0]
    tm = TM_IN
    n_seq_tiles = SEQ // tm
    cd, sd, cg, sg = tabs
    row = lambda w: pl.BlockSpec((tm, w), lambda i: (i, 0))
    col = lambda h: pl.BlockSpec((h, tm), lambda i: (0, i))
    tab = pl.BlockSpec((tm, LANES), lambda i: (i % n_seq_tiles, 0))
    return pl.pallas_call(
        _inproj_kernel,
        grid=(T // tm,),
        in_specs=[row(D_MODEL), _layer_spec(layer, (1, D_MODEL)),
                  _layer_spec(layer, (D_MODEL, IN_COLS)),
                  tab, tab, tab, tab,
                  _layer_spec(layer, (1, 512)), _layer_spec(layer, (1, 128)),
                  _const_spec((512, 512)), _const_spec((128, 128))],
        out_specs=[row(CONV_CH), col(256), row(256), col(DIFF_HEADS * V_AUG), col(512), row(128),
                   col(GQA_KV_HEADS * V_AUG), row(3 * D_MODEL)],
        out_shape=[
            jax.ShapeDtypeStruct((T, CONV_CH), F32),
            jax.ShapeDtypeStruct((256, T), BF16),
            jax.ShapeDtypeStruct((T, 256), BF16),
            jax.ShapeDtypeStruct((DIFF_HEADS * V_AUG, T), BF16),
            jax.ShapeDtypeStruct((512, T), BF16),
            jax.ShapeDtypeStruct((T, 128), BF16),
            jax.ShapeDtypeStruct((GQA_KV_HEADS * V_AUG, T), BF16),
            jax.ShapeDtypeStruct((T, 3 * D_MODEL), BF16),
        ],
        compiler_params=_params(1),
        name="inproj",
    )(x, gpre, w_in, cd, sd, cg, sg, gqn, gkn, bdq, bdk)


def _conv_taps(c, g, pad_ref, w_ref, b_ref, lng_ref, lnb_ref, o_ref, acc_ref, anchor):
    base = c * CONV_ROWS + CONV_PAD - CONV_WIDTH // 2
    k0 = g * CONV_TAPS
    n_taps = min(CONV_TAPS, CONV_WIDTH - k0)
    zero = pltpu.bitcast(
        lax.shift_right_logical(pltpu.bitcast(anchor[:, :CONV_CH], jnp.uint32), jnp.uint32(32)),
        F32)
    w = w_ref[k0:k0 + n_taps, :] + zero
    acc = jnp.broadcast_to(b_ref[...], (CONV_ROWS, CONV_CH)) if g == 0 else acc_ref[...]
    for k in range(n_taps):
        acc = acc + pad_ref[base + k0 + k:base + k0 + k + CONV_ROWS, :] * w[k:k + 1, :]
    if k0 + n_taps < CONV_WIDTH:
        acc_ref[...] = acc
        return
    mu = jnp.mean(acc, axis=-1, keepdims=True)
    d = acc - mu
    var = jnp.mean(d * d, axis=-1, keepdims=True)
    y = d * lax.rsqrt(var + EPS) * lng_ref[...] + lnb_ref[...]
    o_ref[c * CONV_ROWS:(c + 1) * CONV_ROWS, :] = (y * jax.nn.sigmoid(y)).astype(BF16)


def _attention_passes(passes, o_ref, side_work):
    n_chunks = SEQ // KEY_CHUNK
    rows = lambda c: slice(c * KEY_CHUNK, (c + 1) * KEY_CHUNK)
    steps = [(i, c) for i in range(len(passes)) for c in range(n_chunks)]
    stride = len(steps) // len(side_work)
    wqs = {}

    def scores(step):
        i, c = step
        k_ref, make_wq, _, _ = passes[i]
        if c == 0:
            wqs[i] = make_wq()
        return jnp.dot(k_ref[rows(c), :], wqs[i], preferred_element_type=F32)

    s_next = scores(steps[0])
    m = None
    for j, (i, c) in enumerate(steps):
        s = s_next
        if j + 1 < len(steps):
            s_next = scores(steps[j + 1])
        _, _, vT_ref, v0 = passes[i]
        cm = jnp.max(s, axis=0, keepdims=True)
        m_new = cm if c == 0 else jnp.maximum(m, cm)
        p = jnp.exp2(s - m_new).astype(BF16)
        part = jnp.dot(vT_ref[v0:v0 + V_AUG, rows(c)], p, preferred_element_type=F32)
        if c == 0:
            o_ref[i] = part
        else:
            o_ref[i] = o_ref[i] * jnp.exp2(m - m_new) + part
        m = m_new
        if (j + 1) % stride == 0 and (j + 1) // stride <= len(side_work):
            side_work[(j + 1) // stride - 1](m_new)


def _place_rows(block, lo, total):
    rows, n = block.shape
    parts = []
    if lo > 0:
        parts.append(jnp.zeros((lo, n), block.dtype))
    parts.append(block)
    if total - lo - rows > 0:
        parts.append(jnp.zeros((total - lo - rows, n), block.dtype))
    return jnp.concatenate(parts, axis=0) if len(parts) > 1 else block


def _attn_kernel(lam_init, lam_ref, subg_ref, dqT_ref, dk_ref, dvT_ref, gqT_ref, gk_ref, gvT_ref,
                 u_ref, up_ref, un_ref, cw_ref, cb_ref, lng_ref, lnb_ref,
                 ob_ref, oc_ref, ua_ref, o_ref, pad_ref, acc_ref):
    j = pl.program_id(1)
    pad_ref[0:CONV_PAD, :] = jnp.where(j == 0, 0.0, up_ref[...])
    pad_ref[CONV_PAD:CONV_PAD + TQ, :] = u_ref[...]
    pad_ref[CONV_PAD + TQ:, :] = jnp.where(j == pl.num_programs(1) - 1, 0.0, un_ref[...])
    conv_work = [functools.partial(_conv_taps, c, g, pad_ref, cw_ref, cb_ref, lng_ref, lnb_ref,
                                   ua_ref, acc_ref)
                 for c in range(TQ // CONV_ROWS) for g in range(-(-CONV_WIDTH // CONV_TAPS))]

    passes = []
    for h in range(DIFF_HEADS):
        for m in range(2):
            r0 = (2 * h + m) * DIFF_QK_DIM
            make_wq = functools.partial(
                lambda r0: _place_rows(dqT_ref[r0:r0 + DIFF_QK_DIM, :], r0,
                                       2 * DIFF_HEADS * DIFF_QK_DIM), r0)
            passes.append((dk_ref, make_wq, dvT_ref, h * V_AUG))
    per_group = GQA_HEADS // GQA_KV_HEADS
    for h in range(GQA_HEADS):
        g = h // per_group
        make_wq = functools.partial(
            lambda h, g: _place_rows(gqT_ref[h * GQA_HEAD_DIM:(h + 1) * GQA_HEAD_DIM, :],
                                     g * GQA_HEAD_DIM, GQA_KV_HEADS * GQA_HEAD_DIM), h, g)
        passes.append((gk_ref, make_wq, gvT_ref, g * V_AUG))
    _attention_passes(passes, o_ref, conv_work)
    outs = []
    for i in range(len(passes)):
        o = o_ref[i]
        outs.append(o[:64] * (1.0 / o[64:65]))

    lp = lam_ref[...]
    lam = (jnp.exp(jnp.sum(lp[0:1] * lp[1:2], keepdims=True))
           - jnp.exp(jnp.sum(lp[2:3] * lp[3:4], keepdims=True)) + lam_init)
    heads = []
    for h in range(DIFF_HEADS):
        o = outs[2 * h] - lam * outs[2 * h + 1]
        ms = jnp.mean(o * o, axis=0, keepdims=True)
        heads.append(o * lax.rsqrt(ms + EPS) * subg_ref[...] * (1.0 - lam_init))
    ob_ref[...] = jnp.concatenate(heads, axis=0).T.astype(BF16)
    oc_ref[...] = jnp.concatenate(outs[2 * DIFF_HEADS:], axis=0).T.astype(BF16)


def _attention(layer, lam_init, lam_p, subg, dqT, dk, dvT, gqT, gk, gvT, u, cw, cb, lng, lnb):
    T = dk.shape[0]
    nq = SEQ // TQ
    per = TQ // CONV_PAD
    n_halo_blocks = T // CONV_PAD
    qcol = lambda h: pl.BlockSpec((h, TQ), lambda b, j: (0, b * nq + j))
    krow = lambda w: pl.BlockSpec((SEQ, w), lambda b, j: (b, 0))
    vcol = lambda h: pl.BlockSpec((h, SEQ), lambda b, j: (0, b))
    orow = lambda w: pl.BlockSpec((TQ, w), lambda b, j: (b * nq + j, 0))
    prev = pl.BlockSpec((CONV_PAD, CONV_CH),
                        lambda b, j: (jnp.maximum((b * nq + j) * per - 1, 0), 0))
    nxt = pl.BlockSpec((CONV_PAD, CONV_CH),
                       lambda b, j: (jnp.minimum((b * nq + j + 1) * per, n_halo_blocks - 1), 0))
    return pl.pallas_call(
        functools.partial(_attn_kernel, lam_init),
        grid=(T // SEQ, nq),
        in_specs=[_layer_spec(layer, (4, DIFF_QK_DIM)), _layer_spec(layer, (DIFF_V_DIM, 1)),
                  qcol(256), krow(256), vcol(DIFF_HEADS * V_AUG),
                  qcol(512), krow(128), vcol(GQA_KV_HEADS * V_AUG),
                  orow(CONV_CH), prev, nxt,
                  _layer_spec(layer, (CONV_WIDTH, CONV_CH)), _layer_spec(layer, (1, CONV_CH)),
                  _layer_spec(layer, (1, CONV_CH)), _layer_spec(layer, (1, CONV_CH))],
        out_specs=[orow(256), orow(512), orow(CONV_CH)],
        out_shape=[jax.ShapeDtypeStruct((T, 256), BF16), jax.ShapeDtypeStruct((T, 512), BF16),
                   jax.ShapeDtypeStruct((T, CONV_CH), BF16)],
        scratch_shapes=[pltpu.VMEM((2 * DIFF_HEADS + GQA_HEADS, V_AUG, TQ), F32),
                        pltpu.VMEM((TQ + 2 * CONV_PAD, CONV_CH), F32),
                        pltpu.VMEM((CONV_ROWS, CONV_CH), F32)],
        compiler_params=_params(2),
        name="attention",
    )(lam_p, subg, dqT, dk, dvT, gqT, gk, gvT, u, u, u, cw, cb, lng, lnb)


def _merge_kernel(x_ref, ua_ref, ob_ref, oc_ref, gates_ref, wc_ref, wd_ref, wg_ref, wo_ref,
                  gpost_ref, o_ref):
    def gate(j):
        return gates_ref[:, j * D_MODEL:(j + 1) * D_MODEL].astype(F32)

    merged = gate(0) * jnp.dot(ua_ref[...], wc_ref[...], preferred_element_type=F32)
    merged += gate(1) * jnp.dot(ob_ref[...], wd_ref[...], preferred_element_type=F32)
    merged += gate(2) * jnp.dot(oc_ref[...], wg_ref[...], preferred_element_type=F32)
    y = jnp.dot(merged.astype(BF16), wo_ref[...], preferred_element_type=F32)
    o_ref[...] = x_ref[...] + _rms_rows(y, gpost_ref[...])


def _merge(layer, x, ua, ob, oc, gates, wc, wd, wg, wo, gpost):
    T = x.shape[0]
    tm = TM_MERGE
    row = lambda w: pl.BlockSpec((tm, w), lambda i: (i, 0))
    return pl.pallas_call(
        _merge_kernel,
        grid=(T // tm,),
        in_specs=[row(D_MODEL), row(256), row(256), row(512), row(3 * D_MODEL),
                  _layer_spec(layer, (256, D_MODEL)), _layer_spec(layer, (256, D_MODEL)),
                  _layer_spec(layer, (512, D_MODEL)), _layer_spec(layer, (D_MODEL, D_MODEL)),
                  _layer_spec(layer, (1, D_MODEL))],
        out_specs=row(D_MODEL),
        out_shape=jax.ShapeDtypeStruct((T, D_MODEL), F32),
        compiler_params=_params(1),
        name="merge",
    )(x, ua, ob, oc, gates, wc, wd, wg, wo, gpost)


def _ffn_kernel(x_ref, xp_ref, xn_ref, p_ref, gpre_ref, wup_ref, dww_ref, dwb_ref, wdown_ref,
                gpost_ref, wple_ref, wpg_ref, o_ref, acc_ref):
    tm = TM_FFN
    i = pl.program_id(0)
    n_seq_tiles = SEQ // tm
    first = (i % n_seq_tiles) == 0
    last = (i % n_seq_tiles) == n_seq_tiles - 1
    x = x_ref[...]
    gpre = gpre_ref[...]
    hp = jnp.where(first, 0.0, _rms_rows(xp_ref[...], gpre))
    hn = jnp.where(last, 0.0, _rms_rows(xn_ref[...], gpre))
    h2 = jnp.concatenate([hp, _rms_rows(x, gpre), hn], axis=0).astype(BF16)

    def up_proj(c):
        return tuple(
            jnp.dot(h2, wup_ref[:, c0 + c * FFN_CHUNK:c0 + (c + 1) * FFN_CHUNK],
                    preferred_element_type=F32) for c0 in (0, FFN_DIM))

    def conv3(up, c0):
        cols = slice(c0, c0 + FFN_CHUNK)
        w = dww_ref[:, cols]
        prv = pltpu.roll(up, 1, 0)[HALO:HALO + tm]
        nxt = pltpu.roll(up, up.shape[0] - 1, 0)[HALO:HALO + tm]
        return prv * w[0:1] + up[HALO:HALO + tm] * w[1:2] + nxt * w[2:3] + dwb_ref[:, cols]

    n_chunks = FFN_DIM // FFN_CHUNK
    up_next = up_proj(0)
    for c in range(n_chunks):
        up_g, up_v = up_next
        if c + 1 < n_chunks:
            up_next = up_proj(c + 1)
        f_gate = conv3(up_g, c * FFN_CHUNK)
        f_val = conv3(up_v, FFN_DIM + c * FFN_CHUNK)
        act = (jax.nn.gelu(f_gate, approximate=True) * f_val).astype(BF16)
        part = jnp.dot(act, wdown_ref[c * FFN_CHUNK:(c + 1) * FFN_CHUNK, :],
                       preferred_element_type=F32)
        if c == 0:
            acc_ref[...] = part
        else:
            acc_ref[...] += part

    x2 = x + _rms_rows(acc_ref[...], gpost_ref[...])
    gate = jax.nn.sigmoid(jnp.dot(x2.astype(BF16), wpg_ref[...], preferred_element_type=F32))
    emb = jnp.dot(p_ref[...].astype(BF16), wple_ref[...], preferred_element_type=F32)
    o_ref[...] = x2 + gate * emb


def _ffn(layer, x, p, gpre, wup, dww, dwb, wdown, gpost, wple, wpg):
    T = x.shape[0]
    tm = TM_FFN
    per = tm // HALO
    n_halo_blocks = T // HALO
    row = lambda w: pl.BlockSpec((tm, w), lambda i: (i, 0))
    prev = pl.BlockSpec((HALO, D_MODEL), lambda i: (jnp.maximum(i * per - 1, 0), 0))
    nxt = pl.BlockSpec((HALO, D_MODEL),
                       lambda i: (jnp.minimum((i + 1) * per, n_halo_blocks - 1), 0))
    return pl.pallas_call(
        _ffn_kernel,
        grid=(T // tm,),
        in_specs=[row(D_MODEL), prev, nxt,
                  pl.BlockSpec((None, tm, PLE_DIM), lambda i: (layer, i, 0)),
                  _layer_spec(layer, (1, D_MODEL)),
                  _layer_spec(layer, (D_MODEL, 2 * FFN_DIM)), _layer_spec(layer, (3, 2 * FFN_DIM)),
                  _layer_spec(layer, (1, 2 * FFN_DIM)), _layer_spec(layer, (FFN_DIM, D_MODEL)),
                  _layer_spec(layer, (1, D_MODEL)), _layer_spec(layer, (PLE_DIM, D_MODEL)),
                  _layer_spec(layer, (D_MODEL, D_MODEL))],
        out_specs=row(D_MODEL),
        out_shape=jax.ShapeDtypeStruct((T, D_MODEL), F32),
        scratch_shapes=[pltpu.VMEM((tm, D_MODEL), F32)],
        compiler_params=_params(1),
        name="ffn",
    )(x, x, x, p, gpre, wup, dww, dwb, wdown, gpost, wple, wpg)


def _inv_freq(dim):
    return 1.0 / (ROPE_THETA ** (jnp.arange(0, dim, 2, dtype=F32) / dim))


def _rope_tables():
    t = jnp.arange(SEQ, dtype=F32)
    ang_1d = t[:, None] * _inv_freq(DIFF_QK_DIM)[None, :]
    rows = SEQ // GRID_W
    row = jnp.repeat(jnp.arange(rows, dtype=F32), GRID_W)
    col = jnp.tile(jnp.arange(GRID_W, dtype=F32), rows)
    fr = _inv_freq(GQA_HEAD_DIM // 2)
    ang_2d = jnp.concatenate([row[:, None] * fr[None, :], col[:, None] * fr[None, :]], axis=-1)

    def lanes(ang):
        per_head = jnp.repeat(ang, 2, axis=1)
        reps = LANES // per_head.shape[1]
        cos = jnp.tile(jnp.cos(per_head), (1, reps))
        sign = jnp.tile(jnp.array([-1.0, 1.0], F32), LANES // 2)[None, :]
        sin = jnp.tile(jnp.sin(per_head), (1, reps)) * sign
        return cos, sin

    cd, sd = lanes(ang_1d)
    cg, sg = lanes(ang_2d)
    return cd, sd, cg, sg


def _block_diag_mean(n, group):
    idx = jnp.arange(n) // group
    return jnp.where(idx[:, None] == idx[None, :], 1.0 / group, 0.0).astype(BF16)


def kernel(x, p, norm_mix_pre, norm_mix_post, w_in, conv_dw_w, conv_dw_b, conv_ln_g, conv_ln_b, w_conv_out, diff_lambda, diff_subln_g, w_diff_out, gqa_q_norm, gqa_k_norm, w_gqa_out, w_out, norm_ffn_pre, norm_ffn_post, w_up, ffn_dw_w, ffn_dw_b, w_down, w_ple, w_ple_gate):
    B, S, D = x.shape
    depth = w_in.shape[0]
    assert (S, D) == (SEQ, D_MODEL)
    T = B * S
    tabs = _rope_tables()
    bdq = _block_diag_mean(GQA_HEADS * GQA_HEAD_DIM, GQA_HEAD_DIM)
    bdk = _block_diag_mean(GQA_KV_HEADS * GQA_HEAD_DIM, GQA_HEAD_DIM)
    xf = x.reshape(T, D)
    bf = lambda a: a.astype(BF16)
    row3 = lambda a: a.reshape(depth, 1, -1)
    pf = p.reshape(depth, T, PLE_DIM)
    g_mix_pre, g_mix_post = row3(norm_mix_pre), row3(norm_mix_post)
    g_ffn_pre, g_ffn_post = row3(norm_ffn_pre), row3(norm_ffn_post)
    gqn = row3(jnp.tile(gqa_q_norm, (1, GQA_HEADS)))
    gkn = row3(jnp.tile(gqa_k_norm, (1, GQA_KV_HEADS)))
    subg = diff_subln_g.reshape(depth, DIFF_V_DIM, 1)
    cb, lng, lnb = row3(conv_dw_b), row3(conv_ln_g), row3(conv_ln_b)
    wc, wd, wg, wo = bf(w_conv_out), bf(w_diff_out), bf(w_gqa_out), bf(w_out)
    wup, wdown, wple, wpg = bf(w_up), bf(w_down), bf(w_ple), bf(w_ple_gate)
    dwb = row3(ffn_dw_b)
    for i in range(depth):
        lam_init = 0.8 - 0.6 * math.exp(-0.3 * i)
        u, dqT, dk, dvT, gqT, gk, gvT, gates = _inproj(
            i, xf, g_mix_pre, w_in, tabs, gqn, gkn, bdq, bdk)
        ob, oc, ua = _attention(i, lam_init, diff_lambda, subg, dqT, dk, dvT, gqT, gk, gvT,
                                u, conv_dw_w, cb, lng, lnb)
        xf = _merge(i, xf, ua, ob, oc, gates, wc, wd, wg, wo, g_mix_post)
        xf = _ffn(i, xf, pf, g_ffn_pre, wup, ffn_dw_w, dwb, wdown, g_ffn_post, wple, wpg)
    return xf.reshape(B, S, D)
```

```python
import functools
import math

import jax
import jax.numpy as jnp
from jax import lax
from jax.experimental import pallas as pl
from jax.experimental.pallas import tpu as pltpu

F32 = jnp.float32
BF16 = jnp.bfloat16

D_MODEL = 1024
SEQ = 2048
PLE_DIM = 256
GRID_W = 64
ROPE_THETA = 10000.0
EPS = 1e-6
CONV_CH = 256
CONV_WIDTH = 31
DIFF_HEADS = 4
DIFF_QK_DIM = 32
DIFF_V_DIM = 64
GQA_HEADS = 8
GQA_KV_HEADS = 2
GQA_HEAD_DIM = 64
FFN_DIM = 2816

C_CONV = 0
C_DQ = 512
C_DK = 768
C_DV = 1024
C_GQ = 1280
C_GK = 1792
C_GV = 1920
C_GATES = 2048
IN_COLS = 5120

LANES = 128
SUBLANES = 8
VMEM_LIMIT = 56 * 1024 * 1024

TM_IN = 512
TQ = 512
KEY_CHUNK = 512
SCORE_LOOKAHEAD = 3
ONES_ROWS = 16
V_AUG = 64 + ONES_ROWS
LOG2E = math.log2(math.e)
TM_MERGE = 512
TM_FFN = 512
FFN_CHUNK = 256
HALO = SUBLANES
CONV_ROWS = 32
CONV_TAPS = 8
CONV_PAD = 16


def _const_spec(shape):
    nd = len(shape)
    return pl.BlockSpec(shape, lambda *_: (0,) * nd, pipeline_mode=pl.Buffered(1))


def _layer_spec(layer, shape):
    nd = len(shape)
    return pl.BlockSpec((None,) + tuple(shape), lambda *_: (layer,) + (0,) * nd,
                        pipeline_mode=pl.Buffered(1))


def _params(n_axes):
    return pltpu.CompilerParams(
        dimension_semantics=("arbitrary",) * n_axes, vmem_limit_bytes=VMEM_LIMIT)


def _rms_rows(x, g):
    ms = jnp.mean(x * x, axis=-1, keepdims=True)
    return x * lax.rsqrt(ms + EPS) * g


def _rope_lanes(v, cos, sin_signed):
    outs = []
    for j in range(v.shape[1] // LANES):
        c = v[:, j * LANES:(j + 1) * LANES]
        nxt = pltpu.roll(c, LANES - 1, 1)
        prv = pltpu.roll(c, 1, 1)
        lane = lax.broadcasted_iota(jnp.int32, c.shape, 1)
        swapped = jnp.where(lane % 2 == 0, nxt, prv)
        outs.append(c * cos + swapped * sin_signed)
    return outs[0] if len(outs) == 1 else jnp.concatenate(outs, axis=1)


def _head_rms_t(raw, roped, heads):
    raw_t, roped_t = raw.T, roped.T
    outs = []
    for h in range(heads):
        blk = raw_t[h * GQA_HEAD_DIM:(h + 1) * GQA_HEAD_DIM]
        ms = jnp.mean(blk * blk, axis=0, keepdims=True)
        outs.append(roped_t[h * GQA_HEAD_DIM:(h + 1) * GQA_HEAD_DIM] * lax.rsqrt(ms + EPS))
    return jnp.concatenate(outs, axis=0)


def _with_ones_rows(vT, heads):
    ones = jnp.ones((ONES_ROWS, vT.shape[1]), vT.dtype)
    parts = []
    for h in range(heads):
        parts += [vT[h * 64:(h + 1) * 64], ones]
    return jnp.concatenate(parts, axis=0)


def _inproj_kernel(x_ref, gpre_ref, w_ref, cd_ref, sd_ref, cg_ref, sg_ref, gqn_ref, gkn_ref,
                   u_ref, dqT_ref, dk_ref, dvT_ref, gqT_ref, gk_ref, gvT_ref, gates_ref):
    h = _rms_rows(x_ref[...], gpre_ref[...]).astype(BF16)

    def glu(a):
        u_ref[...] = a[:, :CONV_CH] * jax.nn.sigmoid(a[:, CONV_CH:])

    def diff_q(a):
        dq = _rope_lanes(a, cd_ref[...], sd_ref[...]) * (DIFF_QK_DIM ** -0.5 * LOG2E)
        dqT_ref[...] = dq.T.astype(BF16)

    def diff_k(a):
        dk_ref[...] = _rope_lanes(a, cd_ref[...], sd_ref[...]).astype(BF16)

    def diff_v(a):
        dvT_ref[...] = _with_ones_rows(a.T.astype(BF16), DIFF_HEADS)

    def gqa_q(a):
        roped = _rope_lanes(a * gqn_ref[...], cg_ref[...], sg_ref[...])
        gqT_ref[...] = (_head_rms_t(a, roped, GQA_HEADS)
                        * (GQA_HEAD_DIM ** -0.5 * LOG2E)).astype(BF16)

    def gqa_k(a):
        roped = _rope_lanes(a * gkn_ref[...], cg_ref[...], sg_ref[...])
        gk_ref[...] = _head_rms_t(a, roped, GQA_KV_HEADS).T.astype(BF16)

    def gqa_v(a):
        gvT_ref[...] = _with_ones_rows(a.T.astype(BF16), GQA_KV_HEADS)

    def gate(j, a):
        gates_ref[:, j * D_MODEL:(j + 1) * D_MODEL] = jax.nn.sigmoid(a).astype(BF16)

    segments = [(C_CONV, C_DQ, glu), (C_DQ, C_DK, diff_q), (C_DK, C_DV, diff_k),
                (C_DV, C_GQ, diff_v), (C_GQ, C_GK, gqa_q), (C_GK, C_GV, gqa_k),
                (C_GV, C_GATES, gqa_v)]
    segments += [(C_GATES + j * D_MODEL, C_GATES + (j + 1) * D_MODEL, functools.partial(gate, j))
                 for j in range(3)]

    def mm(seg):
        return jnp.dot(h, w_ref[:, seg[0]:seg[1]].astype(BF16), preferred_element_type=F32)

    nxt = mm(segments[0])
    for j, seg in enumerate(segments):
        cur = nxt
        if j + 1 < len(segments):
            nxt = mm(segments[j + 1])
        seg[2](cur)


def _inproj(layer, x, gpre, w_in, tabs, gqn, gkn):
    T = x.shape[0]
    tm = TM_IN
    n_seq_tiles = SEQ // tm
    cd, sd, cg, sg = tabs
    row = lambda w: pl.BlockSpec((tm, w), lambda i: (i, 0))
    col = lambda h: pl.BlockSpec((h, tm), lambda i: (0, i))
    tab = pl.BlockSpec((tm, LANES), lambda i: (i % n_seq_tiles, 0))
    return pl.pallas_call(
        _inproj_kernel,
        grid=(T // tm,),
        in_specs=[row(D_MODEL), _layer_spec(layer, (1, D_MODEL)),
                  _layer_spec(layer, (D_MODEL, IN_COLS)),
                  tab, tab, tab, tab,
                  _layer_spec(layer, (1, 512)), _layer_spec(layer, (1, 128))],
        out_specs=[row(CONV_CH), col(256), row(256), col(DIFF_HEADS * V_AUG), col(512), row(128),
                   col(GQA_KV_HEADS * V_AUG), row(3 * D_MODEL)],
        out_shape=[
            jax.ShapeDtypeStruct((T, CONV_CH), F32),
            jax.ShapeDtypeStruct((256, T), BF16),
            jax.ShapeDtypeStruct((T, 256), BF16),
            jax.ShapeDtypeStruct((DIFF_HEADS * V_AUG, T), BF16),
            jax.ShapeDtypeStruct((512, T), BF16),
            jax.ShapeDtypeStruct((T, 128), BF16),
            jax.ShapeDtypeStruct((GQA_KV_HEADS * V_AUG, T), BF16),
            jax.ShapeDtypeStruct((T, 3 * D_MODEL), BF16),
        ],
        compiler_params=_params(1),
        name="inproj",
    )(x, gpre, w_in, cd, sd, cg, sg, gqn, gkn)


def _conv_taps(c, g, pad_ref, w_ref, b_ref, lng_ref, lnb_ref, o_ref, acc_ref, anchor):
    base = c * CONV_ROWS + CONV_PAD - CONV_WIDTH // 2
    k0 = g * CONV_TAPS
    n_taps = min(CONV_TAPS, CONV_WIDTH - k0)
    zero = pltpu.bitcast(
        lax.shift_right_logical(pltpu.bitcast(anchor[:, :CONV_CH], jnp.uint32), jnp.uint32(32)),
        F32)
    w = w_ref[k0:k0 + n_taps, :] + zero
    acc = jnp.broadcast_to(b_ref[...], (CONV_ROWS, CONV_CH)) if g == 0 else acc_ref[...]
    for k in range(n_taps):
        acc = acc + pad_ref[base + k0 + k:base + k0 + k + CONV_ROWS, :] * w[k:k + 1, :]
    if k0 + n_taps < CONV_WIDTH:
        acc_ref[...] = acc
        return
    mu = jnp.mean(acc, axis=-1, keepdims=True)
    d = acc - mu
    var = jnp.mean(d * d, axis=-1, keepdims=True)
    y = d * lax.rsqrt(var + EPS) * lng_ref[...] + lnb_ref[...]
    o_ref[c * CONV_ROWS:(c + 1) * CONV_ROWS, :] = (y * jax.nn.sigmoid(y)).astype(BF16)


def _attention_passes(passes, o_ref, s_ref, side_work):
    n_chunks = SEQ // KEY_CHUNK
    rows = lambda c: slice(c * KEY_CHUNK, (c + 1) * KEY_CHUNK)
    steps = [(i, c) for i in range(len(passes)) for c in range(n_chunks)]
    stride = len(steps) // len(side_work)
    wqs = {}

    zero = jnp.minimum(pl.program_id(0), 0)

    def scores(j):
        i, c = steps[j]
        k_ref, make_wq, _, _ = passes[i]
        if c == 0:
            wqs[i] = make_wq()
        s = jnp.dot(k_ref[rows(c), :], wqs[i], preferred_element_type=F32)
        s_ref[j % (SCORE_LOOKAHEAD + 1) + zero] = s
        return jnp.max(s, axis=0, keepdims=True)

    ahead = [scores(j) for j in range(SCORE_LOOKAHEAD)]
    m = None
    for j, (i, c) in enumerate(steps):
        cm = ahead.pop(0)
        if j + SCORE_LOOKAHEAD < len(steps):
            ahead.append(scores(j + SCORE_LOOKAHEAD))
        _, _, vT_ref, v0 = passes[i]
        m_new = cm if c == 0 else jnp.maximum(m, cm)
        p = jnp.exp2(s_ref[j % (SCORE_LOOKAHEAD + 1) + zero] - m_new).astype(BF16)
        part = jnp.dot(vT_ref[v0:v0 + V_AUG, rows(c)], p, preferred_element_type=F32)
        if c == 0:
            o_ref[i] = part
        else:
            o_ref[i] = o_ref[i] * jnp.exp2(m - m_new) + part
        m = m_new
        if (j + 1) % stride == 0 and (j + 1) // stride <= len(side_work):
            side_work[(j + 1) // stride - 1](m_new)


def _place_rows(block, lo, total):
    rows, n = block.shape
    parts = []
    if lo > 0:
        parts.append(jnp.zeros((lo, n), block.dtype))
    parts.append(block)
    if total - lo - rows > 0:
        parts.append(jnp.zeros((total - lo - rows, n), block.dtype))
    return jnp.concatenate(parts, axis=0) if len(parts) > 1 else block


def _attn_kernel(lam_init, lam_ref, subg_ref, dqT_ref, dk_ref, dvT_ref, gqT_ref, gk_ref, gvT_ref,
                 u_ref, up_ref, un_ref, cw_ref, cb_ref, lng_ref, lnb_ref,
                 ob_ref, oc_ref, ua_ref, o_ref, pad_ref, acc_ref, s_ref):
    j = pl.program_id(1)
    pad_ref[0:CONV_PAD, :] = jnp.where(j == 0, 0.0, up_ref[...])
    pad_ref[CONV_PAD:CONV_PAD + TQ, :] = u_ref[...]
    pad_ref[CONV_PAD + TQ:, :] = jnp.where(j == pl.num_programs(1) - 1, 0.0, un_ref[...])
    conv_work = [functools.partial(_conv_taps, c, g, pad_ref, cw_ref, cb_ref, lng_ref, lnb_ref,
                                   ua_ref, acc_ref)
                 for c in range(TQ // CONV_ROWS) for g in range(-(-CONV_WIDTH // CONV_TAPS))]

    passes = []
    for h in range(DIFF_HEADS):
        for m in range(2):
            r0 = (2 * h + m) * DIFF_QK_DIM
            make_wq = functools.partial(
                lambda r0: _place_rows(dqT_ref[r0:r0 + DIFF_QK_DIM, :], r0,
                                       2 * DIFF_HEADS * DIFF_QK_DIM), r0)
            passes.append((dk_ref, make_wq, dvT_ref, h * V_AUG))
    per_group = GQA_HEADS // GQA_KV_HEADS
    for h in range(GQA_HEADS):
        g = h // per_group
        make_wq = functools.partial(
            lambda h, g: _place_rows(gqT_ref[h * GQA_HEAD_DIM:(h + 1) * GQA_HEAD_DIM, :],
                                     g * GQA_HEAD_DIM, GQA_KV_HEADS * GQA_HEAD_DIM), h, g)
        passes.append((gk_ref, make_wq, gvT_ref, g * V_AUG))
    _attention_passes(passes, o_ref, s_ref, conv_work)
    outs = []
    for i in range(len(passes)):
        o = o_ref[i]
        outs.append(o[:64] * (1.0 / o[64:65]))

    lp = lam_ref[...]
    lam = (jnp.exp(jnp.sum(lp[0:1] * lp[1:2], keepdims=True))
           - jnp.exp(jnp.sum(lp[2:3] * lp[3:4], keepdims=True)) + lam_init)
    heads = []
    for h in range(DIFF_HEADS):
        o = outs[2 * h] - lam * outs[2 * h + 1]
        ms = jnp.mean(o * o, axis=0, keepdims=True)
        heads.append(o * lax.rsqrt(ms + EPS) * subg_ref[...] * (1.0 - lam_init))
    ob_ref[...] = jnp.concatenate(heads, axis=0).T.astype(BF16)
    oc_ref[...] = jnp.concatenate(outs[2 * DIFF_HEADS:], axis=0).T.astype(BF16)


def _attention(layer, lam_init, lam_p, subg, dqT, dk, dvT, gqT, gk, gvT, u, cw, cb, lng, lnb):
    T = dk.shape[0]
    nq = SEQ // TQ
    per = TQ // CONV_PAD
    n_halo_blocks = T // CONV_PAD
    qcol = lambda h: pl.BlockSpec((h, TQ), lambda b, j: (0, b * nq + j))
    krow = lambda w: pl.BlockSpec((SEQ, w), lambda b, j: (b, 0))
    vcol = lambda h: pl.BlockSpec((h, SEQ), lambda b, j: (0, b))
    orow = lambda w: pl.BlockSpec((TQ, w), lambda b, j: (b * nq + j, 0))
    prev = pl.BlockSpec((CONV_PAD, CONV_CH),
                        lambda b, j: (jnp.maximum((b * nq + j) * per - 1, 0), 0))
    nxt = pl.BlockSpec((CONV_PAD, CONV_CH),
                       lambda b, j: (jnp.minimum((b * nq + j + 1) * per, n_halo_blocks - 1), 0))
    return pl.pallas_call(
        functools.partial(_attn_kernel, lam_init),
        grid=(T // SEQ, nq),
        in_specs=[_layer_spec(layer, (4, DIFF_QK_DIM)), _layer_spec(layer, (DIFF_V_DIM, 1)),
                  qcol(256), krow(256), vcol(DIFF_HEADS * V_AUG),
                  qcol(512), krow(128), vcol(GQA_KV_HEADS * V_AUG),
                  orow(CONV_CH), prev, nxt,
                  _layer_spec(layer, (CONV_WIDTH, CONV_CH)), _layer_spec(layer, (1, CONV_CH)),
                  _layer_spec(layer, (1, CONV_CH)), _layer_spec(layer, (1, CONV_CH))],
        out_specs=[orow(256), orow(512), orow(CONV_CH)],
        out_shape=[jax.ShapeDtypeStruct((T, 256), BF16), jax.ShapeDtypeStruct((T, 512), BF16),
                   jax.ShapeDtypeStruct((T, CONV_CH), BF16)],
        scratch_shapes=[pltpu.VMEM((2 * DIFF_HEADS + GQA_HEADS, V_AUG, TQ), F32),
                        pltpu.VMEM((TQ + 2 * CONV_PAD, CONV_CH), F32),
                        pltpu.VMEM((CONV_ROWS, CONV_CH), F32),
                        pltpu.VMEM((SCORE_LOOKAHEAD + 1, KEY_CHUNK, TQ), F32)],
        compiler_params=_params(2),
        name="attention",
    )(lam_p, subg, dqT, dk, dvT, gqT, gk, gvT, u, u, u, cw, cb, lng, lnb)


def _merge_kernel(x_ref, ua_ref, ob_ref, oc_ref, gates_ref, wc_ref, wd_ref, wg_ref, wo_ref,
                  gpost_ref, o_ref):
    def gate(j):
        return gates_ref[:, j * D_MODEL:(j + 1) * D_MODEL].astype(F32)

    merged = gate(0) * jnp.dot(ua_ref[...], wc_ref[...], preferred_element_type=F32)
    merged += gate(1) * jnp.dot(ob_ref[...], wd_ref[...], preferred_element_type=F32)
    merged += gate(2) * jnp.dot(oc_ref[...], wg_ref[...], preferred_element_type=F32)
    y = jnp.dot(merged.astype(BF16), wo_ref[...], preferred_element_type=F32)
    o_ref[...] = x_ref[...] + _rms_rows(y, gpost_ref[...])


def _merge(layer, x, ua, ob, oc, gates, wc, wd, wg, wo, gpost):
    T = x.shape[0]
    tm = TM_MERGE
    row = lambda w: pl.BlockSpec((tm, w), lambda i: (i, 0))
    return pl.pallas_call(
        _merge_kernel,
        grid=(T // tm,),
        in_specs=[row(D_MODEL), row(256), row(256), row(512), row(3 * D_MODEL),
                  _layer_spec(layer, (256, D_MODEL)), _layer_spec(layer, (256, D_MODEL)),
                  _layer_spec(layer, (512, D_MODEL)), _layer_spec(layer, (D_MODEL, D_MODEL)),
                  _layer_spec(layer, (1, D_MODEL))],
        out_specs=row(D_MODEL),
        out_shape=jax.ShapeDtypeStruct((T, D_MODEL), F32),
        compiler_params=_params(1),
        name="merge",
    )(x, ua, ob, oc, gates, wc, wd, wg, wo, gpost)


def _ffn_kernel(x_ref, xp_ref, xn_ref, p_ref, gpre_ref, wup_ref, dww_ref, dwb_ref, wdown_ref,
                gpost_ref, wple_ref, wpg_ref, o_ref, acc_ref):
    tm = TM_FFN
    i = pl.program_id(0)
    n_seq_tiles = SEQ // tm
    first = (i % n_seq_tiles) == 0
    last = (i % n_seq_tiles) == n_seq_tiles - 1
    x = x_ref[...]
    gpre = gpre_ref[...]
    hp = jnp.where(first, 0.0, _rms_rows(xp_ref[...], gpre))
    hn = jnp.where(last, 0.0, _rms_rows(xn_ref[...], gpre))
    h2 = jnp.concatenate([hp, _rms_rows(x, gpre), hn], axis=0).astype(BF16)

    def up_proj(c):
        return tuple(
            jnp.dot(h2, wup_ref[:, c0 + c * FFN_CHUNK:c0 + (c + 1) * FFN_CHUNK],
                    preferred_element_type=F32) for c0 in (0, FFN_DIM))

    def conv3(up, c0):
        cols = slice(c0, c0 + FFN_CHUNK)
        w = dww_ref[:, cols]
        prv = pltpu.roll(up, 1, 0)[HALO:HALO + tm]
        nxt = pltpu.roll(up, up.shape[0] - 1, 0)[HALO:HALO + tm]
        return prv * w[0:1] + up[HALO:HALO + tm] * w[1:2] + nxt * w[2:3] + dwb_ref[:, cols]

    n_chunks = FFN_DIM // FFN_CHUNK
    up_next = up_proj(0)
    for c in range(n_chunks):
        up_g, up_v = up_next
        if c + 1 < n_chunks:
            up_next = up_proj(c + 1)
        f_gate = conv3(up_g, c * FFN_CHUNK)
        f_val = conv3(up_v, FFN_DIM + c * FFN_CHUNK)
        act = (jax.nn.gelu(f_gate, approximate=True) * f_val).astype(BF16)
        part = jnp.dot(act, wdown_ref[c * FFN_CHUNK:(c + 1) * FFN_CHUNK, :],
                       preferred_element_type=F32)
        if c == 0:
            acc_ref[...] = part
        else:
            acc_ref[...] += part

    x2 = x + _rms_rows(acc_ref[...], gpost_ref[...])
    gate = jax.nn.sigmoid(jnp.dot(x2.astype(BF16), wpg_ref[...], preferred_element_type=F32))
    emb = jnp.dot(p_ref[...].astype(BF16), wple_ref[...], preferred_element_type=F32)
    o_ref[...] = x2 + gate * emb


def _ffn(layer, x, p, gpre, wup, dww, dwb, wdown, gpost, wple, wpg):
    T = x.shape[0]
    tm = TM_FFN
    per = tm // HALO
    n_halo_blocks = T // HALO
    row = lambda w: pl.BlockSpec((tm, w), lambda i: (i, 0))
    prev = pl.BlockSpec((HALO, D_MODEL), lambda i: (jnp.maximum(i * per - 1, 0), 0))
    nxt = pl.BlockSpec((HALO, D_MODEL),
                       lambda i: (jnp.minimum((i + 1) * per, n_halo_blocks - 1), 0))
    return pl.pallas_call(
        _ffn_kernel,
        grid=(T // tm,),
        in_specs=[row(D_MODEL), prev, nxt,
                  pl.BlockSpec((None, tm, PLE_DIM), lambda i: (layer, i, 0)),
                  _layer_spec(layer, (1, D_MODEL)),
                  _layer_spec(layer, (D_MODEL, 2 * FFN_DIM)), _layer_spec(layer, (3, 2 * FFN_DIM)),
                  _layer_spec(layer, (1, 2 * FFN_DIM)), _layer_spec(layer, (FFN_DIM, D_MODEL)),
                  _layer_spec(layer, (1, D_MODEL)), _layer_spec(layer, (PLE_DIM, D_MODEL)),
                  _layer_spec(layer, (D_MODEL, D_MODEL))],
        out_specs=row(D_MODEL),
        out_shape=jax.ShapeDtypeStruct((T, D_MODEL), F32),
        scratch_shapes=[pltpu.VMEM((tm, D_MODEL), F32)],
        compiler_params=_params(1),
        name="ffn",
    )(x, x, x, p, gpre, wup, dww, dwb, wdown, gpost, wple, wpg)


def _inv_freq(dim):
    return 1.0 / (ROPE_THETA ** (jnp.arange(0, dim, 2, dtype=F32) / dim))


def _rope_tables():
    t = jnp.arange(SEQ, dtype=F32)
    ang_1d = t[:, None] * _inv_freq(DIFF_QK_DIM)[None, :]
    rows = SEQ // GRID_W
    row = jnp.repeat(jnp.arange(rows, dtype=F32), GRID_W)
    col = jnp.tile(jnp.arange(GRID_W, dtype=F32), rows)
    fr = _inv_freq(GQA_HEAD_DIM // 2)
    ang_2d = jnp.concatenate([row[:, None] * fr[None, :], col[:, None] * fr[None, :]], axis=-1)

    def lanes(ang):
        per_head = jnp.repeat(ang, 2, axis=1)
        reps = LANES // per_head.shape[1]
        cos = jnp.tile(jnp.cos(per_head), (1, reps))
        sign = jnp.tile(jnp.array([-1.0, 1.0], F32), LANES // 2)[None, :]
        sin = jnp.tile(jnp.sin(per_head), (1, reps)) * sign
        return cos, sin

    cd, sd = lanes(ang_1d)
    cg, sg = lanes(ang_2d)
    return cd, sd, cg, sg


def kernel(x, p, norm_mix_pre, norm_mix_post, w_in, conv_dw_w, conv_dw_b, conv_ln_g, conv_ln_b, w_conv_out, diff_lambda, diff_subln_g, w_diff_out, gqa_q_norm, gqa_k_norm, w_gqa_out, w_out, norm_ffn_pre, norm_ffn_post, w_up, ffn_dw_w, ffn_dw_b, w_down, w_ple, w_ple_gate):
    B, S, D = x.shape
    depth = w_in.shape[0]
    assert (S, D) == (SEQ, D_MODEL)
    T = B * S
    tabs = _rope_tables()
    xf = x.reshape(T, D)
    bf = lambda a: a.astype(BF16)
    row3 = lambda a: a.reshape(depth, 1, -1)
    pf = p.reshape(depth, T, PLE_DIM)
    g_mix_pre, g_mix_post = row3(norm_mix_pre), row3(norm_mix_post)
    g_ffn_pre, g_ffn_post = row3(norm_ffn_pre), row3(norm_ffn_post)
    gqn = row3(jnp.tile(gqa_q_norm, (1, GQA_HEADS)))
    gkn = row3(jnp.tile(gqa_k_norm, (1, GQA_KV_HEADS)))
    subg = diff_subln_g.reshape(depth, DIFF_V_DIM, 1)
    cb, lng, lnb = row3(conv_dw_b), row3(conv_ln_g), row3(conv_ln_b)
    wc, wd, wg, wo = bf(w_conv_out), bf(w_diff_out), bf(w_gqa_out), bf(w_out)
    wup, wdown, wple, wpg = bf(w_up), bf(w_down), bf(w_ple), bf(w_ple_gate)
    dwb = row3(ffn_dw_b)
    for i in range(depth):
        lam_init = 0.8 - 0.6 * math.exp(-0.3 * i)
        u, dqT, dk, dvT, gqT, gk, gvT, gates = _inproj(
            i, xf, g_mix_pre, w_in, tabs, gqn, gkn)
        ob, oc, ua = _attention(i, lam_init, diff_lambda, subg, dqT, dk, dvT, gqT, gk, gvT,
                                u, conv_dw_w, cb, lng, lnb)
        xf = _merge(i, xf, ua, ob, oc, gates, wc, wd, wg, wo, g_mix_post)
        xf = _ffn(i, xf, pf, g_ffn_pre, wup, ffn_dw_w, dwb, wdown, g_ffn_post, wple, wpg)
    return xf.reshape(B, S, D)
```
